```python
import jax, jax.numpy as jnp
from jax import lax
import numpy as np

D_MODEL = 1024
BATCH = 4
SEQ = 8192
DEPTH = 1
DEC_BATCH = 128
DEC_SEQ = 1
PAST_LEN = 8192
PAGE_SIZE = 128

HEAD_DIM = 64
NSA_HEADS = 8
NSA_KV_HEADS = 2
NSA_GROUP = NSA_HEADS // NSA_KV_HEADS
SB_HEADS = 4
MEM_HEADS = 4
N_MEM = 256
NSA_BLOCK = 64
NSA_TOPK = 15
NSA_WINDOW = 512
CMP_HIDDEN = 128
WIN_QBLOCK = 128
SB_QBLOCK = 128
ROPE_THETA = 10000.0
N_EXPERTS = 32
TOP_K = 4
D_FF = 1024
SWIGLU_LIMIT = 7.0
SWIGLU_ALPHA = 1.702
MOE_BLOCK = 128
LN_EPS = 1e-5
DEEPNORM_ALPHA = (2.0 * DEPTH) ** 0.25
DEEPNORM_BETA = (8.0 * DEPTH) ** -0.25
SCALE = HEAD_DIM ** -0.5
NSA_W = NSA_HEADS * HEAD_DIM
NSA_KV_W = NSA_KV_HEADS * HEAD_DIM
SB_W = SB_HEADS * HEAD_DIM
MEM_W = MEM_HEADS * HEAD_DIM
MIX_W = NSA_W + SB_W + MEM_W
IN_SIZES = (NSA_W, 6 * NSA_KV_W, 3 * NSA_HEADS, SB_W, SB_W, SB_W, MEM_W)
IN_W = sum(IN_SIZES)
IN_SPLITS = tuple(int(s) for s in np.cumsum(IN_SIZES)[:-1])

kernel_name = "hymba_nsa_stickbreak_moe_step"


def layer_norm(x, g, b):
    xf = x.astype(jnp.float32)
    mu = jnp.mean(xf, -1, keepdims=True)
    var = jnp.mean(jnp.square(xf - mu), -1, keepdims=True)
    return ((xf - mu) * lax.rsqrt(var + LN_EPS) * g.astype(jnp.float32) + b.astype(jnp.float32)).astype(x.dtype)


def masked_softmax(s, mask):
    s = jnp.where(mask, s.astype(jnp.float32), -jnp.inf)
    m = jnp.max(s, axis=-1, keepdims=True)
    m = jnp.where(jnp.isfinite(m), m, 0.0)
    e = jnp.exp(s - m)
    return e / jnp.maximum(jnp.sum(e, axis=-1, keepdims=True), 1e-30)


def rope(x, pos):
    half = HEAD_DIM // 2
    inv = ROPE_THETA ** (-jnp.arange(half, dtype=jnp.float32) / half)
    ang = pos.astype(jnp.float32)[:, None] * inv[None, :]
    shape = (pos.shape[0],) + (1,) * (x.ndim - 3) + (half,)
    cos = jnp.cos(ang).reshape(shape)
    sin = jnp.sin(ang).reshape(shape)
    xf = x.astype(jnp.float32)
    x1, x2 = xf[..., :half], xf[..., half:]
    return jnp.concatenate([x1 * cos - x2 * sin, x1 * sin + x2 * cos], -1).astype(x.dtype)


def project(x, w_in):
    B, T = x.shape[:2]
    q_nsa, kv_nsa, g_nsa, q_sb, k_sb, v_sb, q_mem = jnp.split(x @ w_in, IN_SPLITS, axis=-1)
    q_nsa = q_nsa.reshape(B, T, NSA_KV_HEADS, NSA_GROUP, HEAD_DIM)
    kv_nsa = kv_nsa.reshape(B, T, 6, NSA_KV_HEADS, HEAD_DIM)
    g_nsa = jax.nn.sigmoid(g_nsa.astype(jnp.float32)).astype(x.dtype).reshape(B, T, NSA_KV_HEADS, NSA_GROUP, 3)
    sb = lambda a: a.reshape(B, T, SB_HEADS, HEAD_DIM)
    return q_nsa, kv_nsa, g_nsa, sb(q_sb), sb(k_sb), sb(v_sb), q_mem.reshape(B, T, MEM_HEADS, HEAD_DIM)


def compress(rows, pe, w1, b1, w2, b2):
    B, L = rows.shape[:2]
    nb = L // NSA_BLOCK
    blk = rows[:, :nb * NSA_BLOCK].reshape(B, nb, NSA_BLOCK, NSA_KV_HEADS, HEAD_DIM) + pe[None, None, :, None, :]
    flat = jnp.transpose(blk, (0, 1, 3, 2, 4)).reshape(B, nb, NSA_KV_HEADS, NSA_BLOCK * HEAD_DIM)
    return jax.nn.gelu(flat @ w1 + b1) @ w2 + b2


def nsa_compressed(q, kc, vc, q_pos):
    nb = kc.shape[1]
    s = jnp.einsum('btkgd,bnkd->btkgn', q, kc) * SCALE
    blk_end = (jnp.arange(nb, dtype=jnp.int32) + 1) * NSA_BLOCK - 1
    vis = blk_end[None, :] <= q_pos[:, None]
    p = masked_softmax(s, vis[None, :, None, None, :])
    return jnp.einsum('btkgn,bnkd->btkgd', p.astype(vc.dtype), vc), p


def select_blocks(p, q_pos):
    nb = p.shape[-1]
    cur = q_pos // NSA_BLOCK
    imp = jnp.sum(p, axis=3)
    cand = jnp.arange(nb, dtype=jnp.int32)[None, :] < cur[:, None]
    imp = jnp.where(cand[None, :, None, :], imp, -1.0)
    _, idx = lax.top_k(imp, min(NSA_TOPK, nb))
    return idx, idx < cur[None, :, None, None]


def nsa_selected(q, k, v, idx, valid, q_pos):
    B, T = q.shape[:2]
    L = k.shape[1]
    cur = jnp.broadcast_to((q_pos // NSA_BLOCK)[None, :, None, None], idx.shape[:3] + (1,)).astype(idx.dtype)
    blocks = jnp.concatenate([idx, cur], -1)
    bvalid = jnp.concatenate([valid, jnp.ones(cur.shape, bool)], -1)
    rows = blocks[..., None] * NSA_BLOCK + jnp.arange(NSA_BLOCK, dtype=idx.dtype)
    mask = (bvalid[..., None] & (rows <= q_pos[None, :, None, None, None])).reshape(B, T, NSA_KV_HEADS, -1)
    rows = jnp.minimum(rows, L - 1).reshape(B, T, NSA_KV_HEADS, -1)
    bi = jnp.arange(B)[:, None, None, None]
    hi = jnp.arange(NSA_KV_HEADS)[None, None, :, None]
    kg = k[bi, rows, hi]
    vg = v[bi, rows, hi]
    s = jnp.einsum('btkgd,btkrd->btkgr', q, kg) * SCALE
    w = masked_softmax(s, mask[:, :, :, None, :])
    return jnp.einsum('btkgr,btkrd->btkgd', w.astype(v.dtype), vg)


def window_attend(q, k, v, q_pos, k_pos):
    d = q_pos[:, None] - k_pos[None, :]
    mask = (d >= 0) & (d < NSA_WINDOW) & (k_pos[None, :] >= 0)
    s = jnp.einsum('btkgd,bskd->btkgs', q, k) * SCALE
    w = masked_softmax(s, mask[None, :, None, None, :])
    return jnp.einsum('btkgs,bskd->btkgd', w.astype(v.dtype), v)


def stick_breaking(q, k, v, q_pos, k_pos):
    z = jnp.einsum('bthd,bshd->bhts', q, k).astype(jnp.float32) * SCALE
    mask = (k_pos[None, :] < q_pos[:, None])[None, None]
    log_beta = jax.nn.log_sigmoid(z)
    log_keep = jnp.where(mask, jax.nn.log_sigmoid(-z), 0.0)
    suffix = lax.cumsum(log_keep, axis=3, reverse=True) - log_keep
    a = jnp.where(mask, jnp.exp(log_beta + suffix), 0.0)
    return jnp.einsum('bhts,bshd->bthd', a.astype(v.dtype), v)


def mem_attend(q, mk, mv):
    s = jnp.einsum('bthd,bmhd->bhtm', q, mk).astype(jnp.float32) * SCALE
    w = jax.nn.softmax(s, axis=-1)
    return jnp.einsum('bhtm,bmhd->bthd', w.astype(mv.dtype), mv)


def nsa_merge(g, o_cmp, o_slc, o_win):
    return g[..., 0:1] * o_cmp + g[..., 1:2] * o_slc + g[..., 2:3] * o_win


def sweep_queries(fn, block, q_pos, *q_args):
    T = q_pos.shape[0]
    nb = T // block
    to_blocks = lambda a: jnp.moveaxis(a.reshape((a.shape[0], nb, block) + a.shape[2:]), 1, 0)
    out = lax.map(lambda xs: fn(*xs), (q_pos.reshape(nb, block),) + tuple(to_blocks(a) for a in q_args))
    return jnp.moveaxis(out, 0, 1).reshape((out.shape[1], T) + out.shape[3:])


def mixer_prompt(x, mem, mw):
    w_in, pe_cmp, w_cmp1, b_cmp1, w_cmp2, b_cmp2, w_mem_kv, w_out = mw
    B, T, _ = x.shape
    pos = jnp.arange(T, dtype=jnp.int32)
    q, kv, g, q_sb, k_sb, v_sb, q_mem = project(x, w_in)
    q_r = rope(q, pos)
    k_cmp, v_cmp, v_slc, v_win = kv[:, :, 0], kv[:, :, 1], kv[:, :, 3], kv[:, :, 5]
    k_slc = rope(kv[:, :, 2], pos)
    k_win = rope(kv[:, :, 4], pos)
    kc = compress(k_cmp, pe_cmp[0], w_cmp1[0], b_cmp1[0], w_cmp2[0], b_cmp2[0])
    vc = compress(v_cmp, pe_cmp[1], w_cmp1[1], b_cmp1[1], w_cmp2[1], b_cmp2[1])
    o_cmp, p_cmp = nsa_compressed(q, kc, vc, pos)
    idx, valid = select_blocks(p_cmp, pos)
    o_slc = sweep_queries(lambda qp, qb, ib, vb: nsa_selected(qb, k_slc, v_slc, ib, vb, qp),
                          NSA_BLOCK, pos, q_r, idx, valid)
    pad = ((0, 0), (NSA_WINDOW, 0), (0, 0), (0, 0))
    k_pad, v_pad = jnp.pad(k_win, pad), jnp.pad(v_win, pad)
    band = NSA_WINDOW + WIN_QBLOCK

    def win_block(qp, qb):
        start = qp[0]
        kb = lax.dynamic_slice_in_dim(k_pad, start, band, axis=1)
        vb = lax.dynamic_slice_in_dim(v_pad, start, band, axis=1)
        kp = start - NSA_WINDOW + jnp.arange(band, dtype=jnp.int32)
        return window_attend(qb, kb, vb, qp, kp)

    o_win = sweep_queries(win_block, WIN_QBLOCK, pos, q_r)
    o_nsa = nsa_merge(g, o_cmp, o_slc, o_win).reshape(B, T, NSA_W)
    o_sb = sweep_queries(lambda qp, qb: stick_breaking(qb, k_sb, v_sb, qp, pos), SB_QBLOCK, pos, q_sb).reshape(B, T, SB_W)
    mkv = (mem @ w_mem_kv).reshape(B, N_MEM, 2, MEM_HEADS, HEAD_DIM)
    o_mem = mem_attend(q_mem, mkv[:, :, 0], mkv[:, :, 1]).reshape(B, T, MEM_W)
    out = jnp.concatenate([o_nsa, o_sb, o_mem], -1) @ w_out
    nsa_rows = jnp.stack([k_cmp, v_cmp, k_slc, v_slc], axis=2)
    win_rows = jnp.stack([k_win, v_win], axis=2)[:, T - min(NSA_WINDOW, T):]
    sb_rows = jnp.stack([k_sb, v_sb], axis=2)
    return out, nsa_rows, win_rows, sb_rows, mkv


def mixer_sample(x, past_nsa, past_sb, win_buf, mem_kv, mw):
    w_in, pe_cmp, w_cmp1, b_cmp1, w_cmp2, b_cmp2, w_mem_kv, w_out = mw
    DB, S, _ = x.shape
    past = past_nsa.shape[1] * PAGE_SIZE
    L = past + S
    pos = past + jnp.arange(S, dtype=jnp.int32)
    q, kv, g, q_sb, k_sb, v_sb, q_mem = project(x, w_in)
    q_r = rope(q, pos)
    nsa_rows = jnp.stack([kv[:, :, 0], kv[:, :, 1], rope(kv[:, :, 2], pos), kv[:, :, 3]], axis=2)
    win_rows = jnp.stack([rope(kv[:, :, 4], pos), kv[:, :, 5]], axis=2)
    sb_rows = jnp.stack([k_sb, v_sb], axis=2)
    nsa_all = jnp.concatenate([past_nsa.reshape((DB, past) + past_nsa.shape[3:]), nsa_rows], 1)
    sb_all = jnp.concatenate([past_sb.reshape((DB, past) + past_sb.shape[3:]), sb_rows], 1)
    kc = compress(nsa_all[:, :, 0], pe_cmp[0], w_cmp1[0], b_cmp1[0], w_cmp2[0], b_cmp2[0])
    vc = compress(nsa_all[:, :, 1], pe_cmp[1], w_cmp1[1], b_cmp1[1], w_cmp2[1], b_cmp2[1])
    o_cmp, p_cmp = nsa_compressed(q, kc, vc, pos)
    idx, valid = select_blocks(p_cmp, pos)
    o_slc = nsa_selected(q_r, nsa_all[:, :, 2], nsa_all[:, :, 3], idx, valid, pos)
    wb = win_buf.shape[1]
    win_all = jnp.concatenate([win_buf, win_rows], 1)
    k_pos_win = past - wb + jnp.arange(wb + S, dtype=jnp.int32)
    o_win = window_attend(q_r, win_all[:, :, 0], win_all[:, :, 1], pos, k_pos_win)
    o_nsa = nsa_merge(g, o_cmp, o_slc, o_win).reshape(DB, S, NSA_W)
    o_sb = stick_breaking(q_sb, sb_all[:, :, 0], sb_all[:, :, 1], pos, jnp.arange(L, dtype=jnp.int32)).reshape(DB, S, SB_W)
    o_mem = mem_attend(q_mem, mem_kv[:, :, 0], mem_kv[:, :, 1]).reshape(DB, S, MEM_W)
    out = jnp.concatenate([o_nsa, o_sb, o_mem], -1) @ w_out
    return out, nsa_rows, win_all[:, S:], sb_rows


def moe(h, w_router, b_router, w_gate_up, b_gate_up, w_down, b_down):
    T, D = h.shape
    logits = (h @ w_router + b_router).astype(jnp.float32)
    top_logit, top_e = lax.top_k(logits, TOP_K)
    gate = jax.nn.softmax(top_logit, axis=-1)
    n_slot = T * TOP_K
    flat_e = top_e.reshape(-1).astype(jnp.int32)
    flat_tok = jnp.arange(n_slot, dtype=jnp.int32) // TOP_K
    order = jnp.argsort(flat_e)
    sorted_e = flat_e[order]
    counts = jnp.bincount(flat_e, length=N_EXPERTS).astype(jnp.int32)
    padded = (counts + MOE_BLOCK - 1) // MOE_BLOCK * MOE_BLOCK
    pad_end = jnp.cumsum(padded)
    pad_start = pad_end - padded
    start = jnp.cumsum(counts) - counts
    dest = pad_start[sorted_e] + jnp.arange(n_slot, dtype=jnp.int32) - start[sorted_e]
    n_blocks = -(-(n_slot + N_EXPERTS * (MOE_BLOCK - 1)) // MOE_BLOCK)
    n_rows = n_blocks * MOE_BLOCK
    row_tok = jnp.full((n_rows,), T, jnp.int32).at[dest].set(flat_tok[order])
    row_gate = jnp.zeros((n_rows,), jnp.float32).at[dest].set(gate.reshape(-1)[order])
    block_e = jnp.minimum(jnp.searchsorted(pad_end, jnp.arange(n_blocks, dtype=jnp.int32) * MOE_BLOCK, side='right'),
                          N_EXPERTS - 1)
    h_pad = jnp.concatenate([h, jnp.zeros((1, D), h.dtype)], 0)
    xs = h_pad[row_tok].reshape(n_blocks, MOE_BLOCK, D)

    def expert_block(args):
        xb, e = args
        gu = xb @ w_gate_up[e] + b_gate_up[e]
        g, u = gu[:, :D_FF], gu[:, D_FF:]
        g = jnp.minimum(g, SWIGLU_LIMIT)
        u = jnp.clip(u, -SWIGLU_LIMIT, SWIGLU_LIMIT)
        act = (u + 1.0) * g * jax.nn.sigmoid(SWIGLU_ALPHA * g)
        return act @ w_down[e] + b_down[e]

    ys = lax.map(expert_block, (xs, block_e)).reshape(n_rows, D)
    y = jax.ops.segment_sum(ys.astype(jnp.float32) * row_gate[:, None], row_tok, num_segments=T + 1)[:T]
    return y.astype(h.dtype)


def post_norm_block(x, mix, fw):
    ln1_g, ln1_b, w_router, b_router, w_gate_up, b_gate_up, w_down, b_down, ln2_g, ln2_b = fw
    h = layer_norm(DEEPNORM_ALPHA * x + mix, ln1_g, ln1_b)
    f = moe(h.reshape(-1, h.shape[-1]), w_router, b_router, w_gate_up, b_gate_up, w_down, b_down).reshape(h.shape)
    return layer_norm(DEEPNORM_ALPHA * h + f, ln2_g, ln2_b)


def setup_inputs(seed: int = 0) -> dict:
    key = jax.random.key(seed)
    ks = jax.random.split(key, 32)
    nrm = lambda k, shape, s=1.0: jax.random.normal(k, shape, jnp.float32) * s
    n_pages = PAST_LEN // PAGE_SIZE
    n_used = DEC_BATCH * n_pages
    n_phys = n_used + max(1, n_used // 4)
    win_len = min(NSA_WINDOW, PAST_LEN)
    page_table = jax.random.permutation(ks[0], n_phys)[:n_used].reshape(DEC_BATCH, n_pages).astype(jnp.int32)
    return {
        'x_prompt': nrm(ks[1], (BATCH, SEQ, D_MODEL)),
        'x_sample': nrm(ks[2], (DEC_BATCH, DEC_SEQ, D_MODEL)),
        'mem_prompt': nrm(ks[3], (BATCH, N_MEM, D_MODEL)),
        'cache_nsa': nrm(ks[4], (DEPTH, n_phys, PAGE_SIZE, 4, NSA_KV_HEADS, HEAD_DIM)),
        'cache_sb': nrm(ks[5], (DEPTH, n_phys, PAGE_SIZE, 2, SB_HEADS, HEAD_DIM)),
        'cache_win': nrm(ks[6], (DEPTH, DEC_BATCH, win_len, 2, NSA_KV_HEADS, HEAD_DIM)),
        'cache_mem': nrm(ks[7], (DEPTH, DEC_BATCH, N_MEM, 2, MEM_HEADS, HEAD_DIM)),
        'page_table': page_table,
        'w_in': nrm(ks[8], (DEPTH, D_MODEL, IN_W), D_MODEL ** -0.5),
        'pe_cmp': nrm(ks[9], (DEPTH, 2, NSA_BLOCK, HEAD_DIM), 0.1),
        'w_cmp1': nrm(ks[10], (DEPTH, 2, NSA_BLOCK * HEAD_DIM, CMP_HIDDEN), (NSA_BLOCK * HEAD_DIM) ** -0.5),
        'b_cmp1': nrm(ks[11], (DEPTH, 2, CMP_HIDDEN), 0.01),
        'w_cmp2': nrm(ks[12], (DEPTH, 2, CMP_HIDDEN, HEAD_DIM), CMP_HIDDEN ** -0.5),
        'b_cmp2': nrm(ks[13], (DEPTH, 2, HEAD_DIM), 0.01),
        'w_mem_kv': nrm(ks[14], (DEPTH, D_MODEL, 2 * MEM_W), D_MODEL ** -0.5),
        'w_out': nrm(ks[15], (DEPTH, MIX_W, D_MODEL), MIX_W ** -0.5 * DEEPNORM_BETA),
        'ln1_g': 1.0 + nrm(ks[16], (DEPTH, D_MODEL), 0.02),
        'ln1_b': nrm(ks[17], (DEPTH, D_MODEL), 0.02),
        'w_router': nrm(ks[18], (DEPTH, D_MODEL, N_EXPERTS), D_MODEL ** -0.5),
        'b_router': nrm(ks[19], (DEPTH, N_EXPERTS), 0.01),
        'w_gate_up': nrm(ks[20], (DEPTH, N_EXPERTS, D_MODEL, 2 * D_FF), D_MODEL ** -0.5),
        'b_gate_up': nrm(ks[21], (DEPTH, N_EXPERTS, 2 * D_FF), 0.01),
        'w_down': nrm(ks[22], (DEPTH, N_EXPERTS, D_FF, D_MODEL), D_FF ** -0.5 * DEEPNORM_BETA),
        'b_down': nrm(ks[23], (DEPTH, N_EXPERTS, D_MODEL), 0.01),
        'ln2_g': 1.0 + nrm(ks[24], (DEPTH, D_MODEL), 0.02),
        'ln2_b': nrm(ks[25], (DEPTH, D_MODEL), 0.02),
    }


def reference(x_prompt, x_sample, mem_prompt, cache_nsa, cache_sb, cache_win, cache_mem, page_table,
              w_in, pe_cmp, w_cmp1, b_cmp1, w_cmp2, b_cmp2, w_mem_kv, w_out, ln1_g, ln1_b,
              w_router, b_router, w_gate_up, b_gate_up, w_down, b_down, ln2_g, ln2_b):
    yp, ys = x_prompt, x_sample
    nsa_p, win_p, sb_p, mem_p, nsa_s, win_s, sb_s = [], [], [], [], [], [], []
    for l in range(DEPTH):
        mw = (w_in[l], pe_cmp[l], w_cmp1[l], b_cmp1[l], w_cmp2[l], b_cmp2[l], w_mem_kv[l], w_out[l])
        fw = (ln1_g[l], ln1_b[l], w_router[l], b_router[l], w_gate_up[l], b_gate_up[l],
              w_down[l], b_down[l], ln2_g[l], ln2_b[l])
        mix_p, r_nsa, r_win, r_sb, r_mem = mixer_prompt(yp, mem_prompt, mw)
        mix_s, s_nsa, s_win, s_sb = mixer_sample(ys, cache_nsa[l, page_table], cache_sb[l, page_table],
                                                 cache_win[l], cache_mem[l], mw)
        yp = post_norm_block(yp, mix_p, fw)
        ys = post_norm_block(ys, mix_s, fw)
        nsa_p.append(r_nsa); win_p.append(r_win); sb_p.append(r_sb); mem_p.append(r_mem)
        nsa_s.append(s_nsa); win_s.append(s_win); sb_s.append(s_sb)
    return (yp, ys, jnp.stack(nsa_p), jnp.stack(win_p), jnp.stack(sb_p), jnp.stack(mem_p),
            jnp.stack(nsa_s), jnp.stack(win_s), jnp.stack(sb_s))
```

```python
import functools

import jax
import jax.numpy as jnp
import numpy as np
from jax import lax
from jax.experimental import pallas as pl
from jax.experimental.pallas import tpu as pltpu

F32 = jnp.float32
BF16 = jnp.bfloat16
I32 = jnp.int32

D_MODEL = 1024
HEAD_DIM = 64
NSA_HEADS = 8
NSA_KV_HEADS = 2
NSA_GROUP = 4
SB_HEADS = 4
MEM_HEADS = 4
N_MEM = 256
NSA_BLOCK = 64
NSA_TOPK = 15
NSA_WINDOW = 512
CMP_HIDDEN = 128
PAGE_SIZE = 128
ROPE_THETA = 10000.0
N_EXPERTS = 32
TOP_K = 4
D_FF = 1024
SWIGLU_LIMIT = 7.0
SWIGLU_ALPHA = 1.702
LN_EPS = 1e-5
DEPTH = 1
DEEPNORM_ALPHA = (2.0 * DEPTH) ** 0.25
SCALE = HEAD_DIM ** -0.5

LANES = 128
NEG = -1e30
M_INIT = -1e20
SB_EXIT = -100.0
VMEM_LIMIT = 56 * 1024 * 1024

C_Q, C_KV, C_QSB, C_KSB, C_VSB, C_QMEM, C_G, C_END = 0, 512, 1280, 1536, 1792, 2048, 2304, 2432


def _cparams(sem):
    return pltpu.CompilerParams(dimension_semantics=sem, vmem_limit_bytes=VMEM_LIMIT)


def _dot_nt(a, b):
    return lax.dot_general(a, b, (((1,), (1,)), ((), ())), preferred_element_type=F32)


def _dot(a, b):
    return jnp.dot(a, b, preferred_element_type=F32)


def _lane(shape):
    return lax.broadcasted_iota(I32, shape, 1)


def _row(shape):
    return lax.broadcasted_iota(I32, shape, 0)


def _proj_kernel(x_ref, w_ref, cos_ref, sin_ref, qc_ref, qr_ref, nsa_ref, win_ref, sb_ref,
                 slc_b_ref, win_b_ref, sb_b_ref, qsb_ref, qmem_ref, gate_ref):
    xb = x_ref[...].astype(BF16)
    cos = cos_ref[...]
    sin = sin_ref[...]
    tm = xb.shape[0]
    first_half = (_lane((tm, LANES)) % HEAD_DIM) < (HEAD_DIM // 2)

    def mm(lo, hi):
        return _dot(xb, w_ref[:, lo:hi])

    def rope(v):
        partner = jnp.where(first_half, pltpu.roll(v, LANES - HEAD_DIM // 2, 1), pltpu.roll(v, HEAD_DIM // 2, 1))
        return v * cos + partner * sin

    q = mm(C_Q, C_KV)
    for c in range(4):
        ch = q[:, c * LANES:(c + 1) * LANES]
        qc_ref[:, c * LANES:(c + 1) * LANES] = (ch * SCALE).astype(BF16)
        qr_ref[:, c * LANES:(c + 1) * LANES] = (rope(ch) * SCALE).astype(BF16)
    kv = mm(C_KV, C_QSB)
    nsa_ref[0, 0:256, :] = kv[:, 0:256].T
    k_slc = rope(kv[:, 256:384])
    v_slc = kv[:, 384:512]
    nsa_ref[0, 256:384, :] = k_slc.T
    nsa_ref[0, 384:512, :] = v_slc.T
    slc_b_ref[:, 0:128] = k_slc.astype(BF16)
    slc_b_ref[:, 128:256] = v_slc.astype(BF16)
    k_win = rope(kv[:, 512:640])
    v_win = kv[:, 640:768]
    win_ref[0, 0:128, :] = k_win.T
    win_ref[0, 128:256, :] = v_win.T
    win_b_ref[:, 0:128] = k_win.astype(BF16)
    win_b_ref[:, 128:256] = v_win.astype(BF16)
    qsb_ref[...] = (mm(C_QSB, C_KSB) * SCALE).astype(BF16)
    sb = mm(C_KSB, C_QMEM)
    sb_ref[0] = sb.T
    sb_b_ref[...] = sb.astype(BF16)
    qmem_ref[...] = (mm(C_QMEM, C_G) * SCALE).astype(BF16)
    gate_ref[...] = jax.nn.sigmoid(mm(C_G, C_END))


def _project(x, w_all, cos_t, sin_t, tm, b, t):
    n = x.shape[0]
    nt = t // tm
    row = lambda w: pl.BlockSpec((tm, w), lambda i: (i, 0))
    tab = pl.BlockSpec((tm, LANES), lambda i: (i % nt, 0))
    outs = [(512, BF16, False), (512, BF16, False), (512, F32, True), (256, F32, True), (512, F32, True),
            (256, BF16, False), (256, BF16, False), (512, BF16, False), (256, BF16, False), (256, BF16, False),
            (128, F32, False)]
    spec = lambda w, fm: pl.BlockSpec((1, w, tm), lambda i: (i // nt, 0, i % nt)) if fm else row(w)
    shape = lambda w, dt, fm: jax.ShapeDtypeStruct((b, w, t) if fm else (n, w), dt)
    return pl.pallas_call(
        _proj_kernel,
        grid=(n // tm,),
        in_specs=[row(D_MODEL), pl.BlockSpec((D_MODEL, C_END), lambda i: (0, 0)), tab, tab],
        out_specs=[spec(w, fm) for w, _, fm in outs],
        out_shape=[shape(*o) for o in outs],
        compiler_params=_cparams(("parallel",)),
        name="proj",
    )(x, w_all, cos_t, sin_t)


def _compress_kernel(pt_ref, rows_hbm, pe_ref, w1_ref, b1_ref, w2_ref, b2_ref, out_ref, buf, sem, *, pg, n_steps, ppb):
    i = pl.program_id(0)
    m = 2 * pg

    def page_copy(step, slot, j, kind):
        page = pt_ref[step * pg + j]
        src = rows_hbm.at[page // ppb, kind * LANES:(kind + 1) * LANES, pl.ds((page % ppb) * PAGE_SIZE, PAGE_SIZE)]
        return pltpu.make_async_copy(src, buf.at[slot, kind, pl.ds(j * LANES, LANES), :], sem.at[slot])

    def issue(step, slot):
        def body(j, c):
            page_copy(step, slot, j, 0).start()
            page_copy(step, slot, j, 1).start()
            return c
        lax.fori_loop(0, pg, body, 0)

    @pl.when(i == 0)
    def _():
        issue(0, 0)

    @pl.when(i + 1 < n_steps)
    def _():
        issue(i + 1, (i + 1) % 2)

    slot = i % 2

    def wait_body(j, c):
        page_copy(i, slot, j, 0).wait()
        page_copy(i, slot, j, 1).wait()
        return c
    lax.fori_loop(0, pg, wait_body, 0)

    def dim_body(dd, accs):
        out = []
        for kind in range(2):
            xk = buf[slot, kind, pl.ds(dd, m, stride=HEAD_DIM), :]
            xk = (xk + pe_ref[kind, pl.ds(dd, 1), :]).astype(BF16)
            out.append(accs[kind] + _dot(xk, w1_ref[kind, dd]))
        return tuple(out)
    zero = jnp.zeros((m, 2 * CMP_HIDDEN), F32)
    acc = lax.fori_loop(0, HEAD_DIM, dim_body, (zero, zero))
    for kind in range(2):
        h = jax.nn.gelu(acc[kind] + b1_ref[kind])
        out_ref[kind] = _dot(h.astype(BF16), w2_ref[kind]) + b2_ref[kind]


def _compress(rows3, page_table, pe_t, w1bd, b1t, w2bd, b2t, pg):
    n_pages = page_table.shape[0]
    n_steps = n_pages // pg
    m = 2 * pg
    const = lambda a: pl.BlockSpec(a.shape, lambda i, pt: (0,) * a.ndim)
    return pl.pallas_call(
        functools.partial(_compress_kernel, pg=pg, n_steps=n_steps, ppb=rows3.shape[2] // PAGE_SIZE),
        grid_spec=pltpu.PrefetchScalarGridSpec(
            num_scalar_prefetch=1,
            grid=(n_steps,),
            in_specs=[pl.BlockSpec(memory_space=pl.ANY), const(pe_t), const(w1bd), const(b1t), const(w2bd), const(b2t)],
            out_specs=pl.BlockSpec((2, m, LANES), lambda i, pt: (0, i, 0)),
            scratch_shapes=[pltpu.VMEM((2, 2, pg * LANES, LANES), F32), pltpu.SemaphoreType.DMA((2,))],
        ),
        out_shape=jax.ShapeDtypeStruct((2, 2 * n_pages, LANES), F32),
        compiler_params=_cparams(("arbitrary",)),
        name="compress",
    )(page_table, rows3, pe_t, w1bd, b1t, w2bd, b2t)


def _blocks_from_compress(out, n_seq):
    n_pages = out.shape[1] // 2
    o = out.reshape(2, n_pages, NSA_KV_HEADS, 2, HEAD_DIM)
    o = jnp.transpose(o, (1, 3, 0, 2, 4))
    return o.reshape(n_seq, 2 * n_pages // n_seq, 256)


def _select_topk(imp, cand, blk):
    tq = imp.shape[0]
    blk_f = blk.astype(F32)
    lane = _lane((tq, LANES))
    impm = jnp.where(cand, imp, -1.0)
    sel = jnp.zeros((tq, LANES), F32)
    idx = jnp.zeros((tq, LANES), F32)
    for r in range(NSA_TOPK):
        mx = jnp.max(impm, axis=1, keepdims=True)
        am = jnp.min(jnp.where(impm == mx, blk_f, float(LANES)), axis=1, keepdims=True)
        hit = blk_f == am
        sel = jnp.where(hit & (mx >= 0.0), 1.0, sel)
        idx = jnp.where(lane == r, am, idx)
        impm = jnp.where(hit, -2.0, impm)
    return sel, idx


def _cmp_kernel(qc_ref, kvc_ref, ocmp_ref, bias_ref, *, tq):
    t0 = pl.program_id(1) * tq
    pos = t0 + _row((tq, LANES))
    blk = _lane((tq, LANES))
    low = blk < HEAD_DIM
    vis = blk * NSA_BLOCK + (NSA_BLOCK - 1) <= pos
    cur = pos >> 6
    cand = blk < cur
    kc = kvc_ref[0, :, 0:128].astype(BF16)
    vc = kvc_ref[0, :, 128:256].astype(BF16)
    imp = [jnp.zeros((tq, LANES), F32), jnp.zeros((tq, LANES), F32)]
    for g in range(NSA_GROUP):
        qch = qc_ref[0, :, g * LANES:(g + 1) * LANES]
        o = []
        for kvh in range(NSA_KV_HEADS):
            qm = jnp.where(low if kvh == 0 else ~low, qch, jnp.zeros_like(qch))
            s = jnp.where(vis, _dot_nt(qm, kc), -jnp.inf)
            mx = jnp.max(s, axis=1, keepdims=True)
            mx = jnp.where(mx > -jnp.inf, mx, 0.0)
            e = jnp.exp(s - mx)
            p = e / jnp.maximum(jnp.sum(e, axis=1, keepdims=True), 1e-30)
            imp[kvh] = imp[kvh] + p
            o.append(_dot(p.astype(BF16), vc))
        ocmp_ref[0, :, g * LANES:(g + 1) * LANES] = jnp.where(low, o[0], o[1])
    for kvh in range(NSA_KV_HEADS):
        sel, _ = _select_topk(imp[kvh], cand, blk)
        keep = (sel > 0.5) | (blk == cur)
        bias_ref[0, :, kvh * LANES:(kvh + 1) * LANES] = jnp.where(keep, 0.0, NEG).astype(BF16)


def _cmp_attend(qc, kvc, tq):
    b, t, _ = qc.shape
    return pl.pallas_call(
        functools.partial(_cmp_kernel, tq=tq),
        grid=(b, t // tq),
        in_specs=[pl.BlockSpec((1, tq, 512), lambda i, j: (i, j, 0)),
                  pl.BlockSpec((1, LANES, 256), lambda i, j: (i, 0, 0))],
        out_specs=[pl.BlockSpec((1, tq, 512), lambda i, j: (i, j, 0)),
                   pl.BlockSpec((1, tq, 256), lambda i, j: (i, j, 0))],
        out_shape=[jax.ShapeDtypeStruct((b, t, 512), F32), jax.ShapeDtypeStruct((b, t, 256), BF16)],
        compiler_params=_cparams(("parallel", "parallel")),
        name="cmp_attend",
    )(qc, kvc)


def _rep(x, k):
    return x if k == 1 else jnp.concatenate([x] * k, axis=1)


def _softmax_step(s, v, m_ref, l_ref, acc_ref):
    k = s.shape[1] // LANES
    m_prev = m_ref[...]
    m_new = jnp.maximum(m_prev, jnp.max(s, axis=1, keepdims=True))
    alpha = jnp.exp(m_prev - m_new)
    p = jnp.exp(s - _rep(m_new, k))
    l_ref[...] = alpha * l_ref[...] + jnp.sum(p, axis=1, keepdims=True)
    acc_ref[...] = alpha * acc_ref[...] + _dot(p.astype(BF16), v)
    m_ref[...] = m_new


def _nsa_kernel(qr_ref, bias_ref, ocmp_ref, gate_ref, slc_ref, win_ref, out_ref, m_ref, l_ref, acc_ref, *, tq, tk):
    j = pl.program_id(1)
    t0 = j * tq
    rows = NSA_GROUP * tq
    low_q = _lane((tq, LANES)) < HEAD_DIM
    qpos = t0 + _row((tq, LANES))
    qpos_rows = jnp.concatenate([qpos] * NSA_GROUP, axis=0)

    def reset():
        m_ref[...] = jnp.full((rows, LANES), M_INIT, F32)
        l_ref[...] = jnp.zeros((rows, LANES), F32)
        acc_ref[...] = jnp.zeros((rows, LANES), F32)

    res = [[None] * NSA_GROUP for _ in range(2)]
    o_slc, o_win = [], []
    for kvh in range(NSA_KV_HEADS):
        keep = low_q if kvh == 0 else ~low_q
        qm = jnp.concatenate(
            [jnp.where(keep, qr_ref[0, :, g * LANES:(g + 1) * LANES], jnp.zeros((tq, LANES), BF16))
             for g in range(NSA_GROUP)], axis=0)
        bias = bias_ref[0, :, kvh * LANES:(kvh + 1) * LANES]
        qaug = jnp.concatenate([qm, jnp.concatenate([bias] * NSA_GROUP, axis=0)], axis=1)

        reset()
        n_tiles = (t0 + tq + tk - 1) // tk

        def slc_body(kt, c):
            k0 = pl.multiple_of(kt * tk, tk)
            kk = slc_ref[0, pl.ds(k0, tk), 0:128]
            vv = slc_ref[0, pl.ds(k0, tk), 128:256]
            kblk = (k0 + _row((tk, LANES))) >> 6
            onehot = jnp.where(kblk == _lane((tk, LANES)), 1.0, 0.0).astype(BF16)
            s = _dot_nt(qaug, jnp.concatenate([kk, onehot], axis=1))
            kpos = k0 + _lane((rows, tk))
            s = jnp.where(kpos <= _rep(qpos_rows, tk // LANES), s, NEG)
            _softmax_step(s, vv, m_ref, l_ref, acc_ref)
            return c
        lax.fori_loop(0, n_tiles, slc_body, 0)
        o_slc.append(acc_ref[...] / l_ref[...])

        reset()
        w0 = jnp.maximum(t0 - NSA_WINDOW, 0)
        n_wt = (t0 - w0) // LANES + tq // LANES

        def win_body(kt, c):
            k0 = pl.multiple_of(w0 + kt * LANES, LANES)
            kk = win_ref[0, pl.ds(k0, LANES), 0:128]
            vv = win_ref[0, pl.ds(k0, LANES), 128:256]
            s = _dot_nt(qm, kk)
            kpos = k0 + _lane((rows, LANES))
            ok = (kpos <= qpos_rows) & (kpos > qpos_rows - NSA_WINDOW)
            s = jnp.where(ok, s, NEG)
            _softmax_step(s, vv, m_ref, l_ref, acc_ref)
            return c
        lax.fori_loop(0, n_wt, win_body, 0)
        o_win.append(acc_ref[...] / l_ref[...])

    gate = gate_ref[...]

    def gate_col(g, c):
        a = gate[:, (0 * NSA_GROUP + g) * 3 + c:(0 * NSA_GROUP + g) * 3 + c + 1]
        b = gate[:, (1 * NSA_GROUP + g) * 3 + c:(1 * NSA_GROUP + g) * 3 + c + 1]
        return jnp.where(low_q, a, b)

    for g in range(NSA_GROUP):
        sl = slice(g * tq, (g + 1) * tq)
        slc = jnp.where(low_q, o_slc[0][sl], o_slc[1][sl])
        win = jnp.where(low_q, o_win[0][sl], o_win[1][sl])
        cmp_ = ocmp_ref[0, :, g * LANES:(g + 1) * LANES]
        out_ref[:, g * LANES:(g + 1) * LANES] = gate_col(g, 0) * cmp_ + gate_col(g, 1) * slc + gate_col(g, 2) * win


def _nsa_attend(qr, bias, ocmp, gate, slc_b, win_b, tq, tk):
    b, t, _ = qr.shape
    nt = t // tq
    rows = NSA_GROUP * tq
    tile = lambda w: pl.BlockSpec((1, tq, w), lambda i, j: (i, j, 0))
    full = lambda w: pl.BlockSpec((1, t, w), lambda i, j: (i, 0, 0))
    flat = lambda w: pl.BlockSpec((tq, w), lambda i, j: (i * nt + j, 0))
    return pl.pallas_call(
        functools.partial(_nsa_kernel, tq=tq, tk=tk),
        grid=(b, nt),
        in_specs=[tile(512), tile(256), tile(512), flat(LANES), full(256), full(256)],
        out_specs=flat(512),
        out_shape=jax.ShapeDtypeStruct((b * t, 512), F32),
        scratch_shapes=[pltpu.VMEM((rows, LANES), F32)] * 3,
        compiler_params=_cparams(("parallel", "arbitrary")),
        name="nsa_attend",
    )(qr, bias, ocmp, gate, slc_b, win_b)


def _log_sigmoid(z):
    return jnp.minimum(z, 0.0) - jnp.log1p(jnp.exp(-jnp.abs(z)))


def _suffix_matrix(tk):
    r = _row((tk, 2 * LANES))
    c = _lane((tk, 2 * LANES))
    return jnp.where((r > c) | (c >= LANES), 1.0, 0.0).astype(BF16)


def _sb_tile(z, mask, carry, u2):
    ls = _log_sigmoid(z)
    lk = jnp.where(mask, ls - z, 0.0)
    hi = lk.astype(BF16)
    lo = (lk - hi.astype(F32)).astype(BF16)
    cs = _dot(hi, u2) + _dot(lo, u2)
    a = jnp.where(mask, jnp.exp(ls + carry + cs[:, :LANES]), 0.0)
    return a.astype(BF16), carry + cs[:, LANES:]


def _sb_kernel(q_ref, kv_ref, out_ref, carry_ref, acc_ref, *, tq):
    j = pl.program_id(1)
    t0 = j * tq
    low = _lane((tq, LANES)) < HEAD_DIM
    qpos = t0 + _row((tq, LANES))
    u2 = _suffix_matrix(LANES)
    qm = []
    for h in range(SB_HEADS):
        ch = q_ref[0, :, (h // 2) * LANES:(h // 2 + 1) * LANES]
        qm.append(jnp.where(low if h % 2 == 0 else ~low, ch, jnp.zeros_like(ch)))
    carry_ref[...] = jnp.zeros_like(carry_ref)
    acc_ref[...] = jnp.zeros_like(acc_ref)

    def cond(state):
        kt, top = state
        return (kt >= 0) & (top > SB_EXIT)

    def body(state):
        kt, _ = state
        k0 = pl.multiple_of(kt * LANES, LANES)
        mask = (k0 + _lane((tq, LANES))) < qpos
        top = jnp.full((), -jnp.inf, F32)
        for h in range(SB_HEADS):
            c = h // 2
            kk = kv_ref[0, pl.ds(k0, LANES), c * LANES:(c + 1) * LANES]
            vv = kv_ref[0, pl.ds(k0, LANES), 256 + c * LANES:256 + (c + 1) * LANES]
            a, carry = _sb_tile(_dot_nt(qm[h], kk), mask, carry_ref[h], u2)
            acc_ref[h] = acc_ref[h] + _dot(a, vv)
            carry_ref[h] = carry
            top = jnp.maximum(top, jnp.max(carry))
        return kt - 1, top

    lax.while_loop(cond, body, (j * (tq // LANES) + tq // LANES - 1, jnp.zeros((), F32)))
    for c in range(SB_HEADS // 2):
        out_ref[:, c * LANES:(c + 1) * LANES] = jnp.where(low, acc_ref[2 * c], acc_ref[2 * c + 1])


def _sb_attend(qsb, sb_b, tq):
    b, t, _ = qsb.shape
    nt = t // tq
    return pl.pallas_call(
        functools.partial(_sb_kernel, tq=tq),
        grid=(b, nt),
        in_specs=[pl.BlockSpec((1, tq, 256), lambda i, j: (i, j, 0)), pl.BlockSpec((1, t, 512), lambda i, j: (i, 0, 0))],
        out_specs=pl.BlockSpec((tq, 256), lambda i, j: (i * nt + j, 0)),
        out_shape=jax.ShapeDtypeStruct((b * t, 256), F32),
        scratch_shapes=[pltpu.VMEM((SB_HEADS, tq, LANES), F32)] * 2,
        compiler_params=_cparams(("parallel", "arbitrary")),
        name="sb_attend",
    )(qsb, sb_b)


def _memkv_kernel(x_ref, w_ref, o_ref, ot_ref):
    y = _dot(x_ref[0].astype(BF16), w_ref[...])
    o_ref[0] = y
    ot_ref[0] = y.T


def _memkv(mem, w):
    b, m, k = mem.shape
    n = w.shape[1]
    return pl.pallas_call(
        _memkv_kernel,
        grid=(b,),
        in_specs=[pl.BlockSpec((1, m, k), lambda i: (i, 0, 0)), pl.BlockSpec(w.shape, lambda i: (0, 0))],
        out_specs=[pl.BlockSpec((1, m, n), lambda i: (i, 0, 0)), pl.BlockSpec((1, n, m), lambda i: (i, 0, 0))],
        out_shape=[jax.ShapeDtypeStruct((b, m, n), F32), jax.ShapeDtypeStruct((b, n, m), F32)],
        compiler_params=_cparams(("parallel",)),
        name="memkv",
    )(mem, w)


def _mem_kernel(q_ref, mkv_ref, out_ref, *, tq):
    low = _lane((tq, LANES)) < HEAD_DIM
    for c in range(MEM_HEADS // 2):
        kk = mkv_ref[0, :, c * LANES:(c + 1) * LANES].astype(BF16)
        vv = mkv_ref[0, :, 256 + c * LANES:256 + (c + 1) * LANES].astype(BF16)
        ch = q_ref[0, :, c * LANES:(c + 1) * LANES]
        o = []
        for half in range(2):
            qm = jnp.where(low if half == 0 else ~low, ch, jnp.zeros_like(ch))
            s = _dot_nt(qm, kk)
            e = jnp.exp(s - jnp.max(s, axis=1, keepdims=True))
            p = e / jnp.sum(e, axis=1, keepdims=True)
            o.append(_dot(p.astype(BF16), vv))
        out_ref[:, c * LANES:(c + 1) * LANES] = jnp.where(low, o[0], o[1])


def _mem_attend(qmem, mkv, tq):
    b, t, _ = qmem.shape
    nt = t // tq
    return pl.pallas_call(
        functools.partial(_mem_kernel, tq=tq),
        grid=(b, nt),
        in_specs=[pl.BlockSpec((1, tq, 256), lambda i, j: (i, j, 0)),
                  pl.BlockSpec((1, N_MEM, 512), lambda i, j: (i, 0, 0))],
        out_specs=pl.BlockSpec((tq, 256), lambda i, j: (i * nt + j, 0)),
        out_shape=jax.ShapeDtypeStruct((b * t, 256), F32),
        compiler_params=_cparams(("parallel", "parallel")),
        name="mem_attend",
    )(qmem, mkv)


def _layer_norm(y, g, b):
    mu = jnp.mean(y, axis=1, keepdims=True)
    d = y - mu
    var = jnp.mean(d * d, axis=1, keepdims=True)
    return d * lax.rsqrt(var + LN_EPS) * g + b


def _outproj_kernel(onsa_ref, osb_ref, omem_ref, x_ref, w_ref, g_ref, b_ref, wr_ref, br_ref, *rest):
    h_ref, route_ref = rest[-2:]
    mix =(_dot(onsa_ref[...].astype(BF16), w_ref[0:512, :]) + _dot(osb_ref[...].astype(BF16), w_ref[512:768, :])
           + _dot(omem_ref[...].astype(BF16), w_ref[768:1024, :]))
    h = _layer_norm(DEEPNORM_ALPHA * x_ref[...] + mix, g_ref[...], b_ref[...])
    h_ref[...] = h
    logits = jnp.dot(h, wr_ref[...], preferred_element_type=F32, precision=lax.Precision.HIGHEST) + br_ref[...]
    tm = h.shape[0]
    lane = _lane((tm, LANES))
    lane_f = lane.astype(F32)
    route = jnp.zeros((tm, LANES), F32)
    es, denom, mx0 = [], 0.0, None
    for r in range(TOP_K):
        mx = jnp.max(logits, axis=1, keepdims=True)
        am = jnp.min(jnp.where(logits == mx, lane_f, float(LANES)), axis=1, keepdims=True)
        logits = jnp.where(lane_f == am, -jnp.inf, logits)
        mx0 = mx if r == 0 else mx0
        e = jnp.exp(mx - mx0)
        es.append(e)
        denom = denom + e
        route = jnp.where(lane == TOP_K + r, am, route)
    for r in range(TOP_K):
        route = jnp.where(lane == r, es[r] / denom, route)
    route_ref[...] = route


def _outproj(onsa, osb, omem, x, w_out_p, ln_g, ln_b, wr, br, n_all, tm, row0, filled=None):
    n = x.shape[0]
    blk0 = row0 // tm
    row = lambda w: pl.BlockSpec((tm, w), lambda i: (i, 0))
    const = lambda a: pl.BlockSpec(a.shape, lambda i: (0,) * a.ndim)
    extra = () if filled is None else tuple(filled)
    return pl.pallas_call(
        _outproj_kernel,
        grid=(n // tm,),
        in_specs=[row(512), row(256), row(256), row(D_MODEL), const(w_out_p), const(ln_g), const(ln_b), const(wr),
                  const(br)] + [pl.BlockSpec(memory_space=pl.ANY)] * len(extra),
        out_specs=[pl.BlockSpec((tm, D_MODEL), lambda i: (blk0 + i, 0)), pl.BlockSpec((tm, LANES), lambda i: (blk0 + i, 0))],
        out_shape=[jax.ShapeDtypeStruct((n_all, D_MODEL), F32), jax.ShapeDtypeStruct((n_all, LANES), F32)],
        input_output_aliases={9: 0, 10: 1} if extra else {},
        compiler_params=_cparams(("parallel",)),
        name="outproj",
    )(onsa, osb, omem, x, w_out_p, ln_g, ln_b, wr, br, *extra)


def _row_gather_pipeline(i, n_steps, idx_hbm, src_hbm, idx, buf, sem_i, sem_x, n_rows):
    slot = i % 2

    def idx_copy(step, s):
        return pltpu.make_async_copy(idx_hbm.at[pl.ds(step, 1), :], idx.at[pl.ds(s, 1), :], sem_i.at[s])

    def gather(s):
        def body(r, c):
            row = idx[s, r]
            pltpu.make_async_copy(src_hbm.at[pl.ds(row, 1), :], buf.at[s, pl.ds(r, 1), :], sem_x.at[s]).start()
            return c
        lax.fori_loop(0, n_rows, body, 0, unroll=8)

    @pl.when(i == 0)
    def _():
        idx_copy(0, 0).start()
        idx_copy(0, 0).wait()
        gather(0)

    @pl.when((i == 0) & (n_steps > 1))
    def _():
        idx_copy(1, 1).start()

    @pl.when(i + 1 < n_steps)
    def _():
        idx_copy(i + 1, 1 - slot).wait()
        gather(1 - slot)

    @pl.when(i + 2 < n_steps)
    def _():
        idx_copy(i + 2, slot).start()

    pltpu.make_async_copy(src_hbm.at[pl.ds(0, n_rows), :], buf.at[slot], sem_x.at[slot]).wait()
    return slot


def _expert_kernel(be_ref, nu_ref, tok_hbm, h_hbm, wgu_ref, bgu_ref, wd_ref, bd_ref, out_ref,
                   xbuf, idx, wgu_b, wd_b, sem_i, sem_x, *, bm, n_blocks):
    i = pl.program_id(0)
    slot = _row_gather_pipeline(i, n_blocks, tok_hbm, h_hbm, idx, xbuf, sem_i, sem_x, bm)

    @pl.when((i == 0) | (be_ref[i] != be_ref[jnp.maximum(i - 1, 0)]))
    def _():
        wgu_b[...] = wgu_ref[0].astype(BF16)
        wd_b[...] = wd_ref[0].astype(BF16)

    @pl.when(i < nu_ref[0])
    def _():
        x = xbuf[slot].astype(BF16)
        gu = _dot(x, wgu_b[...]) + bgu_ref[0]
        g = jnp.minimum(gu[:, :D_FF], SWIGLU_LIMIT)
        u = jnp.clip(gu[:, D_FF:], -SWIGLU_LIMIT, SWIGLU_LIMIT)
        act = (u + 1.0) * g * jax.nn.sigmoid(SWIGLU_ALPHA * g)
        out_ref[...] = _dot(act.astype(BF16), wd_b[...]) + bd_ref[0]

    @pl.when(i >= nu_ref[0])
    def _():
        out_ref[...] = jnp.zeros_like(out_ref)


def _experts(block_e, n_used, row_tok, h_all, w_gate_up, b_gate_up, w_down, b_down, bm):
    n_blocks = row_tok.shape[0]
    wmap = lambda i, be, nu: (be[i], 0, 0)
    return pl.pallas_call(
        functools.partial(_expert_kernel, bm=bm, n_blocks=n_blocks),
        grid_spec=pltpu.PrefetchScalarGridSpec(
            num_scalar_prefetch=2,
            grid=(n_blocks,),
            in_specs=[pl.BlockSpec(memory_space=pl.ANY), pl.BlockSpec(memory_space=pl.ANY),
                      pl.BlockSpec((1, D_MODEL, 2 * D_FF), wmap), pl.BlockSpec((1, 1, 2 * D_FF), wmap),
                      pl.BlockSpec((1, D_FF, D_MODEL), wmap), pl.BlockSpec((1, 1, D_MODEL), wmap)],
            out_specs=pl.BlockSpec((bm, D_MODEL), lambda i, be, nu: (i, 0)),
            scratch_shapes=[pltpu.VMEM((2, bm, D_MODEL), F32), pltpu.SMEM((2, bm), I32),
                            pltpu.VMEM((D_MODEL, 2 * D_FF), BF16), pltpu.VMEM((D_FF, D_MODEL), BF16),
                            pltpu.SemaphoreType.DMA((2,)), pltpu.SemaphoreType.DMA((2,))],
        ),
        out_shape=jax.ShapeDtypeStruct((n_blocks * bm, D_MODEL), F32),
        compiler_params=_cparams(("arbitrary",)),
        name="experts",
    )(block_e, n_used, row_tok, h_all, w_gate_up, b_gate_up.reshape(N_EXPERTS, 1, 2 * D_FF), w_down,
      b_down.reshape(N_EXPERTS, 1, D_MODEL))


def _combine_kernel(pos_hbm, ys_hbm, route_ref, h_ref, g_ref, b_ref, outp_ref, outs_ref, ybuf, idx, sem_i, sem_y,
                    *, tm, n_tiles, n_prompt_tiles):
    i = pl.program_id(0)
    slot = _row_gather_pipeline(i, n_tiles, pos_hbm, ys_hbm, idx, ybuf, sem_i, sem_y, TOP_K * tm)
    route = route_ref[...]
    y = jnp.zeros((tm, D_MODEL), F32)
    for r in range(TOP_K):
        y = y + route[:, r:r + 1] * ybuf[slot, r * tm:(r + 1) * tm, :]
    out = _layer_norm(DEEPNORM_ALPHA * h_ref[...] + y, g_ref[...], b_ref[...])

    @pl.when(i < n_prompt_tiles)
    def _():
        outp_ref[...] = out

    @pl.when(i >= n_prompt_tiles)
    def _():
        outs_ref[...] = out


def _combine(pos_tiles, ys, route_all, h_all, ln_g, ln_b, tm, n_prompt):
    n_all = h_all.shape[0]
    n_tiles = n_all // tm
    n_prompt_tiles = n_prompt // tm
    const = lambda a: pl.BlockSpec(a.shape, lambda i: (0,) * a.ndim)
    return pl.pallas_call(
        functools.partial(_combine_kernel, tm=tm, n_tiles=n_tiles, n_prompt_tiles=n_prompt_tiles),
        grid=(n_tiles,),
        in_specs=[pl.BlockSpec(memory_space=pl.ANY), pl.BlockSpec(memory_space=pl.ANY),
                  pl.BlockSpec((tm, LANES), lambda i: (i, 0)), pl.BlockSpec((tm, D_MODEL), lambda i: (i, 0)),
                  const(ln_g), const(ln_b)],
        out_specs=[pl.BlockSpec((tm, D_MODEL), lambda i: (jnp.minimum(i, n_prompt_tiles - 1), 0)),
                   pl.BlockSpec((tm, D_MODEL), lambda i: (jnp.maximum(i - n_prompt_tiles, 0), 0))],
        out_shape=[jax.ShapeDtypeStruct((n_prompt, D_MODEL), F32),
                   jax.ShapeDtypeStruct((n_all - n_prompt, D_MODEL), F32)],
        scratch_shapes=[pltpu.VMEM((2, TOP_K * tm, D_MODEL), F32), pltpu.SMEM((2, TOP_K * tm), I32),
                        pltpu.SemaphoreType.DMA((2,)), pltpu.SemaphoreType.DMA((2,))],
        compiler_params=_cparams(("arbitrary",)),
        name="combine",
    )(pos_tiles, ys, route_all, h_all, ln_g, ln_b)


def _moe(h_all, route_all, w_gate_up, b_gate_up, w_down, b_down, ln_g, ln_b, n_prompt, bm, tm):
    n_all = h_all.shape[0]
    n_slot = n_all * TOP_K
    flat_e = route_all[:, TOP_K:2 * TOP_K].astype(I32).reshape(-1)
    order = jnp.argsort(flat_e, stable=True).astype(I32)
    sorted_e = flat_e[order]
    counts = jnp.bincount(flat_e, length=N_EXPERTS).astype(I32)
    padded = (counts + bm - 1) // bm * bm
    pad_end = jnp.cumsum(padded)
    pad_start = pad_end - padded
    start = jnp.cumsum(counts) - counts
    dest = pad_start[sorted_e] + jnp.arange(n_slot, dtype=I32) - start[sorted_e]
    n_blocks = -(-(n_slot + N_EXPERTS * (bm - 1)) // bm)
    row_tok = jnp.zeros((n_blocks * bm,), I32).at[dest].set(order // TOP_K)
    slot_pos = jnp.zeros((n_slot,), I32).at[order].set(dest)
    block_e = jnp.minimum(jnp.searchsorted(pad_end, jnp.arange(n_blocks, dtype=I32) * bm, side='right'),
                          N_EXPERTS - 1).astype(I32)
    n_used = (pad_end[-1:] // bm).astype(I32)
    ys = _experts(block_e, n_used, row_tok.reshape(n_blocks, bm), h_all, w_gate_up, b_gate_up, w_down, b_down, bm)
    pos_tiles = jnp.transpose(slot_pos.reshape(n_all // tm, tm, TOP_K), (0, 2, 1)).reshape(n_all // tm, TOP_K * tm)
    return _combine(pos_tiles, ys, route_all, h_all, ln_g, ln_b, tm, n_prompt)


def _cmp_sample_kernel(q8_ref, kvc_ref, ocmp_ref, idx_ref, *, sb, pos):
    blk = _lane((8, LANES))
    vis = blk * NSA_BLOCK + (NSA_BLOCK - 1) <= pos
    imps = []
    for s in range(sb):
        kc = kvc_ref[s, :, 0:128].astype(BF16)
        vc = kvc_ref[s, :, 128:256].astype(BF16)
        sc = jnp.where(vis, _dot_nt(q8_ref[s], kc), -jnp.inf)
        mx = jnp.max(sc, axis=1, keepdims=True)
        mx = jnp.where(mx > -jnp.inf, mx, 0.0)
        e = jnp.exp(sc - mx)
        p = e / jnp.maximum(jnp.sum(e, axis=1, keepdims=True), 1e-30)
        ocmp_ref[s] = _dot(p.astype(BF16), vc)
        for kvh in range(NSA_KV_HEADS):
            imps.append(jnp.sum(p[kvh * NSA_GROUP:(kvh + 1) * NSA_GROUP], axis=0, keepdims=True))
    imp = jnp.concatenate(imps, axis=0)
    blk2 = _lane((2 * sb, LANES))
    _, idx = _select_topk(imp, blk2 < (pos // NSA_BLOCK), blk2)
    idx_ref[0] = idx.astype(I32)


def _cmp_sample(q8c, kvc, sb, pos):
    n = q8c.shape[0]
    return pl.pallas_call(
        functools.partial(_cmp_sample_kernel, sb=sb, pos=pos),
        grid=(n // sb,),
        in_specs=[pl.BlockSpec((sb, 8, LANES), lambda i: (i, 0, 0)), pl.BlockSpec((sb, LANES, 256), lambda i: (i, 0, 0))],
        out_specs=[pl.BlockSpec((sb, 8, LANES), lambda i: (i, 0, 0)), pl.BlockSpec((1, 2 * sb, LANES), lambda i: (i, 0, 0))],
        out_shape=[jax.ShapeDtypeStruct((n, 8, LANES), F32), jax.ShapeDtypeStruct((n // sb, 2 * sb, LANES), I32)],
        compiler_params=_cparams(("parallel",)),
        name="cmp_sample",
    )(q8c, kvc)


def _attend_rows(s, v_t, s_self, v_self):
    mx = jnp.maximum(jnp.max(s, axis=1, keepdims=True), s_self)
    e = jnp.exp(s - mx)
    e_self = jnp.exp(s_self - mx)
    den = jnp.sum(e, axis=1, keepdims=True) + e_self
    num = _dot_nt(e.astype(BF16), v_t) + e_self.astype(BF16).astype(F32) * v_self
    return num / den


def _sample_kernel(pt_ref, sel_ref, q8r_ref, qsb_ref, qmem_ref, nsa_new_ref, win_new_ref, gate_ref, ocmp_ref,
                   cwin_ref, cmem_ref, nsa_hbm, sb_hbm, onsa_ref, osb_ref, omem_ref,
                   slc_buf, sb_buf, sem_slc, sem_sb, *, n_pages):
    i = pl.program_id(0)
    n_sel = NSA_KV_HEADS * NSA_TOPK

    def sel_block(j):
        return sel_ref[i * 2 * 16 + (j // NSA_TOPK) * 16 + j % NSA_TOPK]

    def slc_copy(j):
        page = pt_ref[i * n_pages + sel_block(j) // 2]
        return pltpu.make_async_copy(nsa_hbm.at[page, 256:512, :],
                                     slc_buf.at[j // NSA_TOPK, :, pl.ds((j % NSA_TOPK) * PAGE_SIZE, PAGE_SIZE)],
                                     sem_slc.at[0])

    def sb_copy(p):
        return pltpu.make_async_copy(sb_hbm.at[pt_ref[i * n_pages + p]], sb_buf.at[p % 2], sem_sb.at[p % 2])

    def issue(j, c):
        slc_copy(j).start()
        return c
    lax.fori_loop(0, n_sel, issue, 0)
    sb_copy(n_pages - 1).start()

    q8r = q8r_ref[0]
    row8 = _row((8, LANES))
    lane8 = _lane((8, LANES))
    head_half = (lane8 < HEAD_DIM) == (row8 < NSA_GROUP)

    kw = cwin_ref[0, 0:128, :].astype(BF16)
    vw = cwin_ref[0, 128:256, :].astype(BF16)
    k_new = win_new_ref[0, :, 0:128].astype(BF16).astype(F32)
    v_new = win_new_ref[0, :, 128:256].astype(BF16).astype(F32)
    s = jnp.where(_lane((8, NSA_WINDOW)) >= 1, _dot(q8r, kw), NEG)
    s_self = jnp.sum(q8r.astype(F32) * k_new, axis=1, keepdims=True)
    o_win = _attend_rows(s, vw, s_self, v_new)

    qm = qmem_ref[0]
    sm = _dot(qm, cmem_ref[0, 0:256, :].astype(BF16))
    em = jnp.exp(sm - jnp.max(sm, axis=1, keepdims=True))
    pm = em / jnp.sum(em, axis=1, keepdims=True)
    omem_ref[0] = _dot_nt(pm.astype(BF16), cmem_ref[0, 256:512, :].astype(BF16))

    def wait_slc(j, c):
        slc_copy(j).wait()
        return c
    lax.fori_loop(0, n_sel, wait_slc, 0)
    k_new = nsa_new_ref[0, :, 256:384].astype(BF16).astype(F32)
    v_new = nsa_new_ref[0, :, 384:512].astype(BF16).astype(F32)
    s_self = jnp.sum(q8r.astype(F32) * k_new, axis=1, keepdims=True)
    o_k = []
    for kvh in range(NSA_KV_HEADS):
        in_block = jnp.concatenate(
            [(lane8 // NSA_BLOCK) == (sel_block(kvh * NSA_TOPK + j) % 2) for j in range(NSA_TOPK)], axis=1)
        sc = jnp.where(in_block, _dot(q8r, slc_buf[kvh, 0:128, :].astype(BF16)), NEG)
        o_k.append(_attend_rows(sc, slc_buf[kvh, 128:256, :].astype(BF16), s_self, v_new))
    o_slc = jnp.where(row8 < NSA_GROUP, o_k[0], o_k[1])
    gate = gate_ref[0]
    merged = gate[:, 0:1] * ocmp_ref[0] + gate[:, 1:2] * o_slc + gate[:, 2:3] * o_win
    onsa_ref[0] = jnp.where(head_half, merged, 0.0)

    qs = qsb_ref[0]
    u2 = _suffix_matrix(PAGE_SIZE)
    all_keys = jnp.full((8, PAGE_SIZE), True)

    def cond(state):
        p, top, _, _ = state
        return (p >= 0) & (top > SB_EXIT)

    def body(state):
        p, _, carry, acc = state
        sb_copy(p).wait()

        @pl.when(p > 0)
        def _():
            sb_copy(p - 1).start()
        a, carry = _sb_tile(_dot(qs, sb_buf[p % 2, 0:256, :].astype(BF16)), all_keys, carry, u2)
        contrib = _dot_nt(a, sb_buf[p % 2, 256:512, :].astype(BF16))
        return p - 1, jnp.max(carry[0:SB_HEADS]), carry, acc + contrib

    p_end, _, _, acc = lax.while_loop(
        cond, body, (n_pages - 1, jnp.zeros((), F32), jnp.zeros((8, LANES), F32), jnp.zeros((8, 256), F32)))

    @pl.when(p_end >= 0)
    def _():
        sb_copy(p_end).wait()
    osb_ref[0] = acc


def _sample_attend(page_table, sel, q8r, qsb8, qmem8, nsa_new, win_new, gate8, ocmp8, cache_win, cache_mem,
                   cache_nsa, cache_sb):
    n = q8r.shape[0]
    n_pages = page_table.shape[0] // n
    per = lambda a: pl.BlockSpec((1,) + a.shape[1:], lambda i, pt, sl: (i,) + (0,) * (a.ndim - 1))
    vm = [q8r, qsb8, qmem8, nsa_new, win_new, gate8, ocmp8, cache_win, cache_mem]
    return pl.pallas_call(
        functools.partial(_sample_kernel, n_pages=n_pages),
        grid_spec=pltpu.PrefetchScalarGridSpec(
            num_scalar_prefetch=2,
            grid=(n,),
            in_specs=[per(a) for a in vm] + [pl.BlockSpec(memory_space=pl.ANY)] * 2,
            out_specs=[pl.BlockSpec((1, 8, LANES), lambda i, pt, sl: (i, 0, 0)),
                       pl.BlockSpec((1, 8, 256), lambda i, pt, sl: (i, 0, 0)),
                       pl.BlockSpec((1, 8, 256), lambda i, pt, sl: (i, 0, 0))],
            scratch_shapes=[pltpu.VMEM((NSA_KV_HEADS, 256, NSA_TOPK * PAGE_SIZE), F32),
                            pltpu.VMEM((2, 512, PAGE_SIZE), F32),
                            pltpu.SemaphoreType.DMA((1,)), pltpu.SemaphoreType.DMA((2,))],
        ),
        out_shape=[jax.ShapeDtypeStruct((n, 8, LANES), F32), jax.ShapeDtypeStruct((n, 8, 256), F32),
                   jax.ShapeDtypeStruct((n, 8, 256), F32)],
        compiler_params=_cparams(("arbitrary",)),
        name="sample_attend",
    )(page_table, sel, *vm, cache_nsa, cache_sb)


def _prep_mixer_weights(w_in, pe_cmp, w_cmp1, b_cmp1, w_cmp2, b_cmp2, w_out):
    head_perm = np.array([kvh * NSA_GROUP + g for g in range(NSA_GROUP) for kvh in range(NSA_KV_HEADS)])
    qcols = (head_perm[:, None] * HEAD_DIM + np.arange(HEAD_DIM)[None, :]).reshape(-1)
    w_q = w_in[:, 0:512][:, qcols]
    w_kv = w_in[:, 512:1280]
    w_g = jnp.pad(w_in[:, 1280:1304], ((0, 0), (0, LANES - 24)))
    w_all = jnp.concatenate([w_q, w_kv, w_in[:, 1304:2328], w_g], axis=1).astype(BF16)
    w_out_p = jnp.concatenate([w_out[0:512][qcols], w_out[512:]], axis=0).astype(BF16)
    w1 = jnp.transpose(w_cmp1.reshape(2, NSA_BLOCK, HEAD_DIM, CMP_HIDDEN), (0, 2, 1, 3))
    w1bd = jnp.zeros((2, HEAD_DIM, 2, NSA_BLOCK, 2, CMP_HIDDEN), F32)
    w2bd = jnp.zeros((2, 2, CMP_HIDDEN, 2, HEAD_DIM), F32)
    for blk in range(2):
        w1bd = w1bd.at[:, :, blk, :, blk, :].set(w1)
        w2bd = w2bd.at[:, blk, :, blk, :].set(w_cmp2)
    w1bd = w1bd.reshape(2, HEAD_DIM, LANES, 2 * CMP_HIDDEN).astype(BF16)
    w2bd = w2bd.reshape(2, 2 * CMP_HIDDEN, LANES).astype(BF16)
    pe_t = jnp.tile(jnp.transpose(pe_cmp, (0, 2, 1)), (1, 1, 2))
    b1t = jnp.tile(b_cmp1, (1, 2)).reshape(2, 1, 2 * CMP_HIDDEN)
    b2t = jnp.tile(b_cmp2, (1, 2)).reshape(2, 1, LANES)
    return w_all, w_out_p, (pe_t, w1bd, b1t, w2bd, b2t)


def _rope_tables(pos):
    half = HEAD_DIM // 2
    inv = ROPE_THETA ** (-jnp.arange(half, dtype=F32) / half)
    ang = pos.astype(F32)[:, None] * inv[None, :]
    cos, sin = jnp.cos(ang), jnp.sin(ang)
    cos_t = jnp.concatenate([cos, cos, cos, cos], axis=1)
    sin_t = jnp.concatenate([-sin, sin, -sin, sin], axis=1)
    return cos_t, sin_t


def _head_rows(q, n_heads, width):
    n = q.shape[0]
    lane_head = (np.arange(width) // HEAD_DIM)[None, None, :]
    rows = jnp.where(lane_head == np.arange(n_heads)[None, :, None], q[:, None, :], jnp.zeros((), q.dtype))
    return jnp.pad(rows, ((0, 0), (0, 8 - n_heads), (0, 0)))


def _nsa_head_rows(q):
    n = q.shape[0]
    q4 = q.reshape(n, NSA_GROUP, LANES)
    low = (np.arange(LANES) < HEAD_DIM)[None, None, :]
    zero = jnp.zeros((), q.dtype)
    return jnp.concatenate([jnp.where(low, q4, zero), jnp.where(low, zero, q4)], axis=1)


def _diag_heads(o, n_heads):
    lane_head = (np.arange(o.shape[2]) // HEAD_DIM)[None, None, :]
    keep = lane_head == np.arange(8)[None, :, None]
    return jnp.sum(jnp.where(keep, o, 0.0), axis=1)


def kernel(x_prompt, x_sample, mem_prompt, cache_nsa, cache_sb, cache_win, cache_mem, page_table, w_in, pe_cmp, w_cmp1,
           b_cmp1, w_cmp2, b_cmp2, w_mem_kv, w_out, ln1_g, ln1_b, w_router, b_router, w_gate_up, b_gate_up, w_down,
           b_down, ln2_g, ln2_b):
    assert w_in.shape[0] == DEPTH
    b, t, d = x_prompt.shape
    db = x_sample.shape[0]
    n = b * t
    n_pages = page_table.shape[1]
    n_phys = cache_nsa.shape[1]
    past = n_pages * PAGE_SIZE
    w_all, w_out_p, cw = _prep_mixer_weights(w_in[0], pe_cmp[0], w_cmp1[0], b_cmp1[0], w_cmp2[0], b_cmp2[0], w_out[0])
    ln1 = (ln1_g[0].reshape(1, d), ln1_b[0].reshape(1, d))
    ln2 = (ln2_g[0].reshape(1, d), ln2_b[0].reshape(1, d))
    wr = jnp.pad(w_router[0], ((0, 0), (0, LANES - N_EXPERTS)))
    br = jnp.concatenate([b_router[0], jnp.full((LANES - N_EXPERTS,), NEG, F32)]).reshape(1, LANES)

    cos_t, sin_t = _rope_tables(jnp.arange(t, dtype=I32))
    qc, qr, nsa_t, win_t, sb_t, slc_b, win_b, sb_b, qsb, qmem, gate = _project(x_prompt.reshape(n, d), w_all, cos_t,
                                                                               sin_t, 256, b, t)
    kvc = _blocks_from_compress(_compress(nsa_t, jnp.arange(n // PAGE_SIZE, dtype=I32), *cw, pg=64), b)
    ocmp, bias = _cmp_attend(qc.reshape(b, t, 512), kvc, 256)
    onsa = _nsa_attend(qr.reshape(b, t, 512), bias, ocmp, gate, slc_b.reshape(b, t, 256), win_b.reshape(b, t, 256),
                       128, 256)
    osb = _sb_attend(qsb.reshape(b, t, 256), sb_b.reshape(b, t, 512), 128)
    mkv, mkv_t = _memkv(mem_prompt, w_mem_kv[0].astype(BF16))
    omem = _mem_attend(qmem.reshape(b, t, 256), mkv, 256)
    h_all, route_all = _outproj(onsa, osb, omem, x_prompt.reshape(n, d), w_out_p, *ln1, wr, br, n + db, 256, 0)

    feature_major = lambda c: jnp.transpose(c[0], (0, 2, 3, 4, 1)).reshape(c.shape[1], -1, c.shape[2])
    cos_s, sin_s = _rope_tables(jnp.full((db,), past, I32))
    qc_s, qr_s, nsa_ts, win_ts, sb_ts, _, _, _, qsb_s, qmem_s, gate_s = _project(x_sample.reshape(db, d), w_all, cos_s,
                                                                               sin_s, db, 1, db)
    nsa_s, win_s = nsa_ts[0].T, win_ts[0].T
    pages_nsa = feature_major(cache_nsa)
    pages_sb = feature_major(cache_sb)
    pt_flat = page_table.reshape(-1)
    kvc_s = _blocks_from_compress(_compress(pages_nsa, pt_flat, *cw, pg=64), db)
    ocmp8, idx = _cmp_sample(_nsa_head_rows(qc_s), kvc_s, 8, past)
    sel = idx[:, :, :16].reshape(-1)
    gate8 = jnp.pad(gate_s[:, :24].reshape(db, 8, 3), ((0, 0), (0, 0), (0, LANES - 3)))
    onsa8, osb8, omem8 = _sample_attend(
        pt_flat, sel, _nsa_head_rows(qr_s), _head_rows(qsb_s, SB_HEADS, 256), _head_rows(qmem_s, MEM_HEADS, 256),
        nsa_s.reshape(db, 1, 512), win_s.reshape(db, 1, 256), gate8, ocmp8,
        feature_major(cache_win), feature_major(cache_mem), pages_nsa, pages_sb)
    onsa_s = (onsa8[:, :NSA_GROUP] + onsa8[:, NSA_GROUP:]).reshape(db, 512)
    h_all, route_all = _outproj(onsa_s, _diag_heads(osb8, SB_HEADS), _diag_heads(omem8, MEM_HEADS),
                                x_sample.reshape(db, d), w_out_p, *ln1, wr, br, n + db, db, n,
                                filled=(h_all, route_all))

    yp, ys = _moe(h_all, route_all, w_gate_up[0], b_gate_up[0], w_down[0], b_down[0], *ln2, n, 256, 128)

    def rows(x_t, kinds, heads):
        s, _, p = x_t.shape
        return jnp.transpose(x_t.reshape(s, kinds, heads, HEAD_DIM, p), (0, 4, 1, 2, 3))[None]

    win_len = min(NSA_WINDOW, t)
    win_all_t = jnp.concatenate([feature_major(cache_win), win_s[:, :, None]], axis=2)[:, :, 1:]
    return (yp.reshape(b, t, d), ys.reshape(db, 1, d),
            rows(nsa_t, 4, NSA_KV_HEADS), rows(win_t[:, :, t - win_len:], 2, NSA_KV_HEADS),
            rows(sb_t, 2, SB_HEADS), rows(mkv_t, 2, MEM_HEADS),
            jnp.transpose(rows(nsa_ts, 4, NSA_KV_HEADS), (0, 2, 1, 3, 4, 5)), rows(win_all_t, 2, NSA_KV_HEADS),
            jnp.transpose(rows(sb_ts, 2, SB_HEADS), (0, 2, 1, 3, 4, 5)))
```

```python
import functools

import jax
import jax.numpy as jnp
import numpy as np
from jax import lax
from jax.experimental import pallas as pl
from jax.experimental.pallas import tpu as pltpu

F32 = jnp.float32
BF16 = jnp.bfloat16
I32 = jnp.int32

D_MODEL = 1024
HEAD_DIM = 64
NSA_HEADS = 8
NSA_KV_HEADS = 2
NSA_GROUP = 4
SB_HEADS = 4
MEM_HEADS = 4
N_MEM = 256
NSA_BLOCK = 64
NSA_TOPK = 15
NSA_WINDOW = 512
CMP_HIDDEN = 128
PAGE_SIZE = 128
ROPE_THETA = 10000.0
N_EXPERTS = 32
TOP_K = 4
D_FF = 1024
SWIGLU_LIMIT = 7.0
SWIGLU_ALPHA = 1.702
LN_EPS = 1e-5
DEPTH = 1
DEEPNORM_ALPHA = (2.0 * DEPTH) ** 0.25
SCALE = HEAD_DIM ** -0.5

LANES = 128
NEG = -1e30
M_INIT = -1e20
SB_EXIT = -100.0
VMEM_LIMIT = 56 * 1024 * 1024

PROJ_TM = 256
COMPRESS_PG = 64
COMPRESS_DG = 8
CMP_TQ = 256
NSA_TQ = 256
NSA_TK = 512
SB_TQ = 256
MEM_TQ = 256
OUT_TM = 256
MOE_BM = 256
COMBINE_TM = 128
SAMPLE_SB = 8

C_Q, C_KV, C_QSB, C_KSB, C_VSB, C_QMEM, C_G, C_END = 0, 512, 1280, 1536, 1792, 2048, 2304, 2432


def _cparams(sem, manual_dma=False):
    return pltpu.CompilerParams(dimension_semantics=sem, vmem_limit_bytes=VMEM_LIMIT,
                                disable_bounds_checks=manual_dma)


def _dot_nt(a, b):
    return lax.dot_general(a, b, (((1,), (1,)), ((), ())), preferred_element_type=F32)


def _dot(a, b):
    return jnp.dot(a, b, preferred_element_type=F32)


def _lane(shape):
    return lax.broadcasted_iota(I32, shape, 1)


def _row(shape):
    return lax.broadcasted_iota(I32, shape, 0)


def _proj_kernel(x_ref, w_ref, cos_ref, sin_ref, qc_ref, qr_ref, nsa_ref, win_ref, sb_ref,
                 slc_b_ref, win_b_ref, sb_b_ref, qsb_ref, qmem_ref, gate_ref):
    xb = x_ref[...].astype(BF16)
    cos = cos_ref[...]
    sin = sin_ref[...]
    tm = xb.shape[0]
    first_half = (_lane((tm, LANES)) % HEAD_DIM) < (HEAD_DIM // 2)

    def mm(lo, hi):
        return _dot(xb, w_ref[:, lo:hi])

    def rope(v):
        partner = jnp.where(first_half, pltpu.roll(v, LANES - HEAD_DIM // 2, 1), pltpu.roll(v, HEAD_DIM // 2, 1))
        return v * cos + partner * sin

    q = mm(C_Q, C_KV)
    for c in range(4):
        ch = q[:, c * LANES:(c + 1) * LANES]
        qc_ref[:, c * LANES:(c + 1) * LANES] = (ch * SCALE).astype(BF16)
        qr_ref[:, c * LANES:(c + 1) * LANES] = (rope(ch) * SCALE).astype(BF16)
    kv = mm(C_KV, C_QSB)
    nsa_ref[0, 0:256, :] = kv[:, 0:256].T
    k_slc = rope(kv[:, 256:384])
    v_slc = kv[:, 384:512]
    nsa_ref[0, 256:384, :] = k_slc.T
    nsa_ref[0, 384:512, :] = v_slc.T
    slc_b_ref[:, 0:128] = k_slc.astype(BF16)
    slc_b_ref[:, 128:256] = v_slc.astype(BF16)
    k_win = rope(kv[:, 512:640])
    v_win = kv[:, 640:768]
    win_ref[0, 0:128, :] = k_win.T
    win_ref[0, 128:256, :] = v_win.T
    win_b_ref[:, 0:128] = k_win.astype(BF16)
    win_b_ref[:, 128:256] = v_win.astype(BF16)
    qsb_ref[...] = (mm(C_QSB, C_KSB) * SCALE).astype(BF16)
    sb = mm(C_KSB, C_QMEM)
    sb_ref[0] = sb.T
    sb_b_ref[...] = sb.astype(BF16)
    qmem_ref[...] = (mm(C_QMEM, C_G) * SCALE).astype(BF16)
    gate_ref[...] = jax.nn.sigmoid(mm(C_G, C_END))


def _project(x, w_all, cos_t, sin_t, b, t, tm=PROJ_TM):
    n = x.shape[0]
    tm = min(tm, t)
    nt = t // tm
    row = lambda w: pl.BlockSpec((tm, w), lambda i: (i, 0))
    tab = pl.BlockSpec((tm, LANES), lambda i: (i % nt, 0))
    outs = [(512, BF16, False), (512, BF16, False), (512, F32, True), (256, F32, True), (512, F32, True),
            (256, BF16, False), (256, BF16, False), (512, BF16, False), (256, BF16, False), (256, BF16, False),
            (128, F32, False)]
    spec = lambda w, fm: pl.BlockSpec((1, w, tm), lambda i: (i // nt, 0, i % nt)) if fm else row(w)
    shape = lambda w, dt, fm: jax.ShapeDtypeStruct((b, w, t) if fm else (n, w), dt)
    return pl.pallas_call(
        _proj_kernel,
        grid=(n // tm,),
        in_specs=[row(D_MODEL), pl.BlockSpec((D_MODEL, C_END), lambda i: (0, 0)), tab, tab],
        out_specs=[spec(w, fm) for w, _, fm in outs],
        out_shape=[shape(*o) for o in outs],
        compiler_params=_cparams(("parallel",)),
        name="proj",
    )(x, w_all, cos_t, sin_t)


def _compress_kernel(pt_ref, rows_hbm, pe_ref, w1_ref, b1_ref, w2_ref, b2_ref, out_ref, buf, sem, *, pg, n_steps, ppb):
    i = pl.program_id(0)
    m = 2 * pg
    slabs = [(kind, head) for kind in range(2) for head in range(NSA_KV_HEADS)]

    def slab_copy(step, slot, j, kind, head):
        page = pt_ref[step * pg + j]
        src = rows_hbm.at[page // ppb, pl.ds(kind * LANES + head * HEAD_DIM, HEAD_DIM),
                          pl.ds((page % ppb) * PAGE_SIZE, PAGE_SIZE)]
        return pltpu.make_async_copy(src, buf.at[slot, kind, :, NSA_KV_HEADS * j + head, :], sem.at[slot])

    def issue(step, slot):
        def body(j, c):
            for kind, head in slabs:
                slab_copy(step, slot, j, kind, head).start()
            return c
        lax.fori_loop(0, pg, body, 0)

    @pl.when(i == 0)
    def _():
        issue(0, 0)

    @pl.when(i + 1 < n_steps)
    def _():
        issue(i + 1, (i + 1) % 2)

    slot = i % 2

    def wait_body(j, c):
        for kind, head in slabs:
            slab_copy(i, slot, j, kind, head).wait()
        return c
    lax.fori_loop(0, pg, wait_body, 0)

    n_groups = w1_ref.shape[1]
    dg = HEAD_DIM // n_groups
    for kind in range(2):
        acc = b1_ref[kind]
        for g in range(n_groups):
            xk = jnp.concatenate([buf[slot, kind, g * dg + dd] for dd in range(dg)], axis=1)
            acc = acc + _dot((xk + pe_ref[kind, g]).astype(BF16), w1_ref[kind, g])
        h = jax.nn.gelu(acc)
        out_ref[kind] = _dot(h.astype(BF16), w2_ref[kind]) + b2_ref[kind]


def _compress(rows3, page_table, pe_t, w1bd, b1t, w2bd, b2t, pg=COMPRESS_PG):
    n_pages = page_table.shape[0]
    n_steps = n_pages // pg
    m = 2 * pg
    const = lambda a: pl.BlockSpec(a.shape, lambda i, pt: (0,) * a.ndim)
    return pl.pallas_call(
        functools.partial(_compress_kernel, pg=pg, n_steps=n_steps, ppb=rows3.shape[2] // PAGE_SIZE),
        grid_spec=pltpu.PrefetchScalarGridSpec(
            num_scalar_prefetch=1,
            grid=(n_steps,),
            in_specs=[pl.BlockSpec(memory_space=pl.ANY), const(pe_t), const(w1bd), const(b1t), const(w2bd), const(b2t)],
            out_specs=pl.BlockSpec((2, m, LANES), lambda i, pt: (0, i, 0)),
            scratch_shapes=[pltpu.VMEM((2, 2, HEAD_DIM, m, LANES), F32), pltpu.SemaphoreType.DMA((2,))],
        ),
        out_shape=jax.ShapeDtypeStruct((2, 2 * n_pages, LANES), F32),
        compiler_params=_cparams(("arbitrary",)),
        name="compress",
    )(page_table, rows3, pe_t, w1bd, b1t, w2bd, b2t)


def _blocks_from_compress(out, n_seq):
    n_pages = out.shape[1] // 2
    o = out.reshape(2, n_pages, NSA_KV_HEADS, 2, HEAD_DIM)
    o = jnp.transpose(o, (1, 3, 0, 2, 4))
    return o.reshape(n_seq, 2 * n_pages // n_seq, 256)


def _select_topk(imp, cand, blk):
    tq = imp.shape[0]
    blk_f = blk.astype(F32)
    lane = _lane((tq, LANES))
    impm = jnp.where(cand, imp, -1.0)
    sel = jnp.zeros((tq, LANES), F32)
    idx = jnp.zeros((tq, LANES), F32)
    for r in range(NSA_TOPK):
        mx = jnp.max(impm, axis=1, keepdims=True)
        am = jnp.min(jnp.where(impm == mx, blk_f, float(LANES)), axis=1, keepdims=True)
        hit = blk_f == am
        sel = jnp.where(hit & (mx >= 0.0), 1.0, sel)
        idx = jnp.where(lane == r, am, idx)
        impm = jnp.where(hit, -2.0, impm)
    return sel, idx


def _cmp_kernel(qc_ref, kvc_ref, ocmp_ref, bias_ref, *, tq):
    t0 = pl.program_id(1) * tq
    pos = t0 + _row((tq, LANES))
    blk = _lane((tq, LANES))
    low = blk < HEAD_DIM
    vis = blk * NSA_BLOCK + (NSA_BLOCK - 1) <= pos
    cur = pos >> 6
    cand = blk < cur
    kc = kvc_ref[0, :, 0:128].astype(BF16)
    vc = kvc_ref[0, :, 128:256].astype(BF16)
    imp = [jnp.zeros((tq, LANES), F32), jnp.zeros((tq, LANES), F32)]
    for g in range(NSA_GROUP):
        qch = qc_ref[0, :, g * LANES:(g + 1) * LANES]
        o = []
        for kvh in range(NSA_KV_HEADS):
            qm = jnp.where(low if kvh == 0 else ~low, qch, jnp.zeros_like(qch))
            s = jnp.where(vis, _dot_nt(qm, kc), -jnp.inf)
            mx = jnp.max(s, axis=1, keepdims=True)
            mx = jnp.where(mx > -jnp.inf, mx, 0.0)
            e = jnp.exp(s - mx)
            p = e / jnp.maximum(jnp.sum(e, axis=1, keepdims=True), 1e-30)
            imp[kvh] = imp[kvh] + p
            o.append(_dot(p.astype(BF16), vc))
        ocmp_ref[0, :, g * LANES:(g + 1) * LANES] = jnp.where(low, o[0], o[1])
    for kvh in range(NSA_KV_HEADS):
        sel, _ = _select_topk(imp[kvh], cand, blk)
        keep = (sel > 0.5) | (blk == cur)
        bias_ref[0, :, kvh * LANES:(kvh + 1) * LANES] = jnp.where(keep, 0.0, NEG).astype(BF16)


def _cmp_attend(qc, kvc, tq=CMP_TQ):
    b, t, _ = qc.shape
    tq = min(tq, t)
    return pl.pallas_call(
        functools.partial(_cmp_kernel, tq=tq),
        grid=(b, t // tq),
        in_specs=[pl.BlockSpec((1, tq, 512), lambda i, j: (i, j, 0)),
                  pl.BlockSpec((1, LANES, 256), lambda i, j: (i, 0, 0))],
        out_specs=[pl.BlockSpec((1, tq, 512), lambda i, j: (i, j, 0)),
                   pl.BlockSpec((1, tq, 256), lambda i, j: (i, j, 0))],
        out_shape=[jax.ShapeDtypeStruct((b, t, 512), F32), jax.ShapeDtypeStruct((b, t, 256), BF16)],
        compiler_params=_cparams(("parallel", "parallel")),
        name="cmp_attend",
    )(qc, kvc)


def _rep(x, k):
    return x if k == 1 else jnp.concatenate([x] * k, axis=1)


def _softmax_step(s, v_aug, m_ref, acc_ref):
    k = s.shape[1] // LANES
    m_prev = m_ref[...]
    m_new = jnp.maximum(m_prev, jnp.max(s, axis=1, keepdims=True))
    alpha = jnp.exp(m_prev - m_new)
    p = jnp.exp(s - _rep(m_new, k)).astype(BF16)
    acc_ref[...] = _rep(alpha, 2) * acc_ref[...] + _dot(p, v_aug)
    m_ref[...] = m_new


def _nsa_kernel(qr_ref, bias_ref, ocmp_ref, gate_ref, slc_ref, win_ref, out_ref, m_ref, acc_ref, *, tq, tk, band):
    j = pl.program_id(1)
    t = slc_ref.shape[1]
    t0 = j * tq
    rows = NSA_GROUP * tq
    low_q = _lane((tq, LANES)) < HEAD_DIM
    qpos = t0 + _row((tq, LANES))
    qpos_rows = jnp.concatenate([qpos] * NSA_GROUP, axis=0)
    heads = range(NSA_KV_HEADS)

    qm, qaug = [], []
    for kvh in heads:
        keep = low_q if kvh == 0 else ~low_q
        qm.append(jnp.concatenate(
            [jnp.where(keep, qr_ref[0, :, g * LANES:(g + 1) * LANES], jnp.zeros((tq, LANES), BF16))
             for g in range(NSA_GROUP)], axis=0))
        bias = bias_ref[0, :, kvh * LANES:(kvh + 1) * LANES]
        qaug.append(jnp.concatenate([qm[kvh], jnp.concatenate([bias] * NSA_GROUP, axis=0)], axis=1))

    m_ref[...] = jnp.full(m_ref.shape, M_INIT, F32)
    acc_ref[...] = jnp.zeros(acc_ref.shape, F32)
    ones_k = jnp.ones((tk, LANES), BF16)

    def slc_tile(kt, causal):
        k0 = pl.multiple_of(kt * tk, tk)
        kblk = (k0 + _row((tk, LANES))) >> 6
        onehot = jnp.where(kblk == _lane((tk, LANES)), 1.0, 0.0).astype(BF16)
        k_aug = jnp.concatenate([slc_ref[0, pl.ds(k0, tk), 0:128], onehot], axis=1)
        v_aug = jnp.concatenate([slc_ref[0, pl.ds(k0, tk), 128:256], ones_k], axis=1)
        for kvh in heads:
            s = _dot_nt(qaug[kvh], k_aug)
            if causal:
                s = jnp.where(k0 + _lane((rows, tk)) <= _rep(qpos_rows, tk // LANES), s, NEG)
            _softmax_step(s, v_aug, m_ref.at[kvh], acc_ref.at[kvh])

    n_open = t0 // tk
    n_tiles = (t0 + tq + tk - 1) // tk

    def open_body(kt, c):
        slc_tile(kt, False)
        return c

    def diag_body(kt, c):
        slc_tile(kt, True)
        return c
    lax.fori_loop(0, n_open, open_body, 0)
    lax.fori_loop(n_open, n_tiles, diag_body, 0)
    o_slc = [acc_ref[kvh, :, 0:LANES] / acc_ref[kvh, :, LANES:2 * LANES] for kvh in heads]

    w0 = pl.multiple_of(jnp.clip(t0 - NSA_WINDOW, 0, t - band), LANES)
    kw = win_ref[0, pl.ds(w0, band), 0:128]
    vw_aug = jnp.concatenate([win_ref[0, pl.ds(w0, band), 128:256], jnp.ones((band, LANES), BF16)], axis=1)
    kpos = w0 + _lane((rows, band))
    qp = _rep(qpos_rows, band // LANES)
    in_window = (kpos <= qp) & (kpos > qp - NSA_WINDOW)
    o_win = []
    for kvh in heads:
        s = jnp.where(in_window, _dot_nt(qm[kvh], kw), NEG)
        p = jnp.exp(s - jnp.max(s, axis=1, keepdims=True)).astype(BF16)
        o = _dot(p, vw_aug)
        o_win.append(o[:, 0:LANES] / o[:, LANES:2 * LANES])

    gate = gate_ref[...]

    def gate_col(g, c):
        a = gate[:, (0 * NSA_GROUP + g) * 3 + c:(0 * NSA_GROUP + g) * 3 + c + 1]
        b = gate[:, (1 * NSA_GROUP + g) * 3 + c:(1 * NSA_GROUP + g) * 3 + c + 1]
        return jnp.where(low_q, a, b)

    for g in range(NSA_GROUP):
        sl = slice(g * tq, (g + 1) * tq)
        slc = jnp.where(low_q, o_slc[0][sl], o_slc[1][sl])
        win = jnp.where(low_q, o_win[0][sl], o_win[1][sl])
        cmp_ = ocmp_ref[0, :, g * LANES:(g + 1) * LANES]
        out_ref[:, g * LANES:(g + 1) * LANES] = gate_col(g, 0) * cmp_ + gate_col(g, 1) * slc + gate_col(g, 2) * win


def _nsa_attend(qr, bias, ocmp, gate, slc_b, win_b, tq=NSA_TQ, tk=NSA_TK):
    b, t, _ = qr.shape
    tq, tk = min(tq, t), min(tk, t)
    nt = t // tq
    rows = NSA_GROUP * tq
    band = min(NSA_WINDOW + tq, t)
    tile = lambda w: pl.BlockSpec((1, tq, w), lambda i, j: (i, j, 0))
    full = lambda w: pl.BlockSpec((1, t, w), lambda i, j: (i, 0, 0))
    flat = lambda w: pl.BlockSpec((tq, w), lambda i, j: (i * nt + j, 0))
    return pl.pallas_call(
        functools.partial(_nsa_kernel, tq=tq, tk=tk, band=band),
        grid=(b, nt),
        in_specs=[tile(512), tile(256), tile(512), flat(LANES), full(256), full(256)],
        out_specs=flat(512),
        out_shape=jax.ShapeDtypeStruct((b * t, 512), F32),
        scratch_shapes=[pltpu.VMEM((NSA_KV_HEADS, rows, LANES), F32), pltpu.VMEM((NSA_KV_HEADS, rows, 2 * LANES), F32)],
        compiler_params=_cparams(("parallel", "arbitrary")),
        name="nsa_attend",
    )(qr, bias, ocmp, gate, slc_b, win_b)


def _log_sigmoid(z):
    return jnp.minimum(z, 0.0) - jnp.log1p(jnp.exp(-jnp.abs(z)))


def _suffix_matrix(tk):
    r = _row((tk, 2 * LANES))
    c = _lane((tk, 2 * LANES))
    return jnp.where((r > c) | (c >= LANES), 1.0, 0.0).astype(BF16)


def _sb_tile(z, mask, carry, u2):
    ls = _log_sigmoid(z)
    lk = jnp.where(mask, ls - z, 0.0)
    hi = lk.astype(BF16)
    lo = (lk - hi.astype(F32)).astype(BF16)
    cs = _dot(hi, u2) + _dot(lo, u2)
    a = jnp.where(mask, jnp.exp(ls + carry + cs[:, :LANES]), 0.0)
    return a.astype(BF16), carry + cs[:, LANES:]


def _sb_kernel(q_ref, kv_ref, out_ref, carry_ref, acc_ref, *, tq):
    j = pl.program_id(1)
    t0 = j * tq
    rows = SB_HEADS * tq
    low = _lane((tq, LANES)) < HEAD_DIM
    qpos_rows = jnp.concatenate([t0 + _row((tq, LANES))] * SB_HEADS, axis=0)
    u2 = _suffix_matrix(LANES)
    qpair = []
    for c in range(SB_HEADS // 2):
        ch = q_ref[0, :, c * LANES:(c + 1) * LANES]
        zero = jnp.zeros_like(ch)
        qpair.append(jnp.concatenate([jnp.where(low, ch, zero), jnp.where(low, zero, ch)], axis=0))
    carry_ref[...] = jnp.zeros_like(carry_ref)
    acc_ref[...] = jnp.zeros_like(acc_ref)

    def cond(state):
        kt, top = state
        return (kt >= 0) & (top > SB_EXIT)

    def body(state):
        kt, _ = state
        k0 = pl.multiple_of(kt * LANES, LANES)
        mask = (k0 + _lane((rows, LANES))) < qpos_rows
        z = jnp.concatenate([_dot_nt(qpair[c], kv_ref[0, pl.ds(k0, LANES), c * LANES:(c + 1) * LANES])
                             for c in range(SB_HEADS // 2)], axis=0)
        a, carry = _sb_tile(z, mask, carry_ref[...], u2)
        contrib = jnp.concatenate(
            [_dot(a[2 * c * tq:2 * (c + 1) * tq], kv_ref[0, pl.ds(k0, LANES), 256 + c * LANES:256 + (c + 1) * LANES])
             for c in range(SB_HEADS // 2)], axis=0)
        acc_ref[...] = acc_ref[...] + contrib
        carry_ref[...] = carry
        return kt - 1, jnp.max(carry)

    lax.while_loop(cond, body, ((t0 + tq) // LANES - 1, jnp.zeros((), F32)))
    for c in range(SB_HEADS // 2):
        out_ref[:, c * LANES:(c + 1) * LANES] = jnp.where(low, acc_ref[2 * c * tq:(2 * c + 1) * tq],
                                                          acc_ref[(2 * c + 1) * tq:(2 * c + 2) * tq])


def _sb_attend(qsb, sb_b, tq=SB_TQ):
    b, t, _ = qsb.shape
    tq = min(tq, t)
    nt = t // tq
    return pl.pallas_call(
        functools.partial(_sb_kernel, tq=tq),
        grid=(b, nt),
        in_specs=[pl.BlockSpec((1, tq, 256), lambda i, j: (i, j, 0)), pl.BlockSpec((1, t, 512), lambda i, j: (i, 0, 0))],
        out_specs=pl.BlockSpec((tq, 256), lambda i, j: (i * nt + j, 0)),
        out_shape=jax.ShapeDtypeStruct((b * t, 256), F32),
        scratch_shapes=[pltpu.VMEM((SB_HEADS * tq, LANES), F32)] * 2,
        compiler_params=_cparams(("parallel", "arbitrary")),
        name="sb_attend",
    )(qsb, sb_b)


def _memkv_kernel(x_ref, w_ref, o_ref, ot_ref):
    y = _dot(x_ref[0].astype(BF16), w_ref[...])
    o_ref[0] = y
    ot_ref[0] = y.T


def _memkv(mem, w):
    b, m, k = mem.shape
    n = w.shape[1]
    return pl.pallas_call(
        _memkv_kernel,
        grid=(b,),
        in_specs=[pl.BlockSpec((1, m, k), lambda i: (i, 0, 0)), pl.BlockSpec(w.shape, lambda i: (0, 0))],
        out_specs=[pl.BlockSpec((1, m, n), lambda i: (i, 0, 0)), pl.BlockSpec((1, n, m), lambda i: (i, 0, 0))],
        out_shape=[jax.ShapeDtypeStruct((b, m, n), F32), jax.ShapeDtypeStruct((b, n, m), F32)],
        compiler_params=_cparams(("parallel",)),
        name="memkv",
    )(mem, w)


def _mem_kernel(q_ref, mkv_ref, out_ref, *, tq):
    low = _lane((tq, LANES)) < HEAD_DIM
    for c in range(MEM_HEADS // 2):
        kk = mkv_ref[0, :, c * LANES:(c + 1) * LANES].astype(BF16)
        vv = mkv_ref[0, :, 256 + c * LANES:256 + (c + 1) * LANES].astype(BF16)
        ch = q_ref[0, :, c * LANES:(c + 1) * LANES]
        o = []
        for half in range(2):
            qm = jnp.where(low if half == 0 else ~low, ch, jnp.zeros_like(ch))
            s = _dot_nt(qm, kk)
            e = jnp.exp(s - jnp.max(s, axis=1, keepdims=True))
            p = e / jnp.sum(e, axis=1, keepdims=True)
            o.append(_dot(p.astype(BF16), vv))
        out_ref[:, c * LANES:(c + 1) * LANES] = jnp.where(low, o[0], o[1])


def _mem_attend(qmem, mkv, tq=MEM_TQ):
    b, t, _ = qmem.shape
    tq = min(tq, t)
    nt = t // tq
    return pl.pallas_call(
        functools.partial(_mem_kernel, tq=tq),
        grid=(b, nt),
        in_specs=[pl.BlockSpec((1, tq, 256), lambda i, j: (i, j, 0)),
                  pl.BlockSpec((1, N_MEM, 512), lambda i, j: (i, 0, 0))],
        out_specs=pl.BlockSpec((tq, 256), lambda i, j: (i * nt + j, 0)),
        out_shape=jax.ShapeDtypeStruct((b * t, 256), F32),
        compiler_params=_cparams(("parallel", "parallel")),
        name="mem_attend",
    )(qmem, mkv)


def _layer_norm(y, g, b):
    mu = jnp.mean(y, axis=1, keepdims=True)
    d = y - mu
    var = jnp.mean(d * d, axis=1, keepdims=True)
    return d * lax.rsqrt(var + LN_EPS) * g + b


def _outproj_kernel(*refs, n_tiles, tail_rows):
    if tail_rows:
        ins, (h_tail_ref, route_tail_ref, h_ref, route_ref) = refs[:9], refs[9:]
        i = pl.program_id(0)
        pl.when(i < n_tiles)(functools.partial(_outproj_tile, *ins, h_ref, route_ref))

        @pl.when(i == n_tiles)
        def _():
            for src, dst in ((h_tail_ref, h_ref), (route_tail_ref, route_ref)):
                dst[0:tail_rows, :] = src[...]
                dst[tail_rows:, :] = jnp.zeros((dst.shape[0] - tail_rows, dst.shape[1]), F32)
    else:
        _outproj_tile(*refs)


def _outproj_tile(onsa_ref, osb_ref, omem_ref, x_ref, w_ref, g_ref, b_ref, wr_ref, br_ref, h_ref, route_ref):
    mix = (_dot(onsa_ref[...].astype(BF16), w_ref[0:512, :]) + _dot(osb_ref[...].astype(BF16), w_ref[512:768, :])
           + _dot(omem_ref[...].astype(BF16), w_ref[768:1024, :]))
    h = _layer_norm(DEEPNORM_ALPHA * x_ref[...] + mix, g_ref[...], b_ref[...])
    h_ref[...] = h
    logits = jnp.dot(h, wr_ref[...], preferred_element_type=F32, precision=lax.Precision.HIGHEST) + br_ref[...]
    tm = h.shape[0]
    lane = _lane((tm, LANES))
    lane_f = lane.astype(F32)
    route = jnp.zeros((tm, LANES), F32)
    es, denom, mx0 = [], 0.0, None
    for r in range(TOP_K):
        mx = jnp.max(logits, axis=1, keepdims=True)
        am = jnp.min(jnp.where(logits == mx, lane_f, float(LANES)), axis=1, keepdims=True)
        logits = jnp.where(lane_f == am, -jnp.inf, logits)
        mx0 = mx if r == 0 else mx0
        e = jnp.exp(mx - mx0)
        es.append(e)
        denom = denom + e
        route = jnp.where(lane == TOP_K + r, am, route)
    for r in range(TOP_K):
        route = jnp.where(lane == r, es[r] / denom, route)
    route_ref[...] = route


def _outproj(onsa, osb, omem, x, w_out_p, ln_g, ln_b, wr, br, tm=OUT_TM, tail=None):
    n = x.shape[0]
    tm = min(tm, n)
    n_tiles = n // tm
    tail = () if tail is None else tuple(tail)
    tail_rows = tail[0].shape[0] if tail else 0
    assert tail_rows <= tm
    row = lambda w: pl.BlockSpec((tm, w), lambda i: (jnp.minimum(i, n_tiles - 1), 0))
    const = lambda a: pl.BlockSpec(a.shape, lambda i: (0,) * a.ndim)
    n_out = n + (tm if tail else 0)
    return pl.pallas_call(
        functools.partial(_outproj_kernel, n_tiles=n_tiles, tail_rows=tail_rows),
        grid=(n_out // tm,),
        in_specs=[row(512), row(256), row(256), row(D_MODEL), const(w_out_p), const(ln_g), const(ln_b), const(wr),
                  const(br)] + [const(a) for a in tail],
        out_specs=[pl.BlockSpec((tm, D_MODEL), lambda i: (i, 0)), pl.BlockSpec((tm, LANES), lambda i: (i, 0))],
        out_shape=[jax.ShapeDtypeStruct((n_out, D_MODEL), F32), jax.ShapeDtypeStruct((n_out, LANES), F32)],
        compiler_params=_cparams(("parallel",)),
        name="outproj",
    )(onsa, osb, omem, x, w_out_p, ln_g, ln_b, wr, br, *tail)


def _expert_kernel(be_ref, rows_hbm, h_hbm, wgu_ref, bgu_ref, wd_ref, bd_ref, ys_hbm,
                   xbuf, ybuf, idx, wgu_b, wd_b, sem_i, sem_x, sem_y, *, bm, n_blocks, n_tok, plane):
    i = pl.program_id(0)
    slot = i % 2
    n_chunks = D_FF // 256
    per_chunk = bm // n_chunks

    ring = lambda blk: (blk + 3) % 3
    par = lambda blk: (blk + 2) % 2

    def idx_copy(blk):
        b = jnp.minimum(blk, n_blocks - 1)
        return pltpu.make_async_copy(rows_hbm.at[pl.ds(b, 1)], idx.at[pl.ds(ring(blk), 1)], sem_i.at[ring(blk)])

    def gather_row(blk, r):
        return pltpu.make_async_copy(h_hbm.at[pl.ds(idx[ring(blk), 0, r], 1), :], xbuf.at[par(blk), pl.ds(r, 1), :],
                                     sem_x.at[par(blk)])

    def scatter_row(blk, r):
        return pltpu.make_async_copy(ybuf.at[par(blk), pl.ds(r, 1), :], ys_hbm.at[pl.ds(idx[ring(blk), 1, r], 1), :],
                                     sem_y.at[par(blk)])

    def block_bytes_wait(buf, sem, s):
        pltpu.make_async_copy(h_hbm.at[pl.ds(0, bm), :], buf.at[s], sem.at[s]).wait()

    @pl.when(i == 0)
    def _():
        idx_copy(0).start()
        idx_copy(1).start()

        def spare(r, c):
            idx[2, 1, r] = (r % TOP_K) * plane + n_tok + r // TOP_K
            return c
        lax.fori_loop(0, bm, spare, 0)
        ybuf[1] = jnp.zeros((bm, D_MODEL), F32)
        idx_copy(0).wait()

        def first(r, c):
            gather_row(0, r).start()
            return c
        lax.fori_loop(0, bm, first, 0)

    block_bytes_wait(xbuf, sem_x, slot)

    @pl.when((i == 0) | (be_ref[i] != be_ref[jnp.maximum(i - 1, 0)]))
    def _():
        wgu_b[...] = wgu_ref[0].astype(BF16)
        wd_b[...] = wd_ref[0].astype(BF16)

    idx_copy(i + 1).wait()

    @pl.when(i >= 1)
    def _():
        block_bytes_wait(ybuf, sem_y, slot)

    x = xbuf[slot].astype(BF16)
    y = bd_ref[0]
    for c in range(n_chunks):
        cols = slice(c * 256, (c + 1) * 256)
        ucols = slice(D_FF + c * 256, D_FF + (c + 1) * 256)
        g = jnp.minimum(_dot(x, wgu_b[:, cols]) + bgu_ref[0, :, cols], SWIGLU_LIMIT)
        u = jnp.clip(_dot(x, wgu_b[:, ucols]) + bgu_ref[0, :, ucols], -SWIGLU_LIMIT, SWIGLU_LIMIT)
        act = (u + 1.0) * g * jax.nn.sigmoid(SWIGLU_ALPHA * g)
        y = y + _dot(act.astype(BF16), wd_b[cols, :])
        for r in range(c * per_chunk, (c + 1) * per_chunk):
            gather_row(i + 1, r).start()
            scatter_row(i - 1, r).start()
    ybuf[slot] = y
    idx_copy(i + 2).start()

    @pl.when(i == n_blocks - 1)
    def _():
        def last(r, c):
            scatter_row(i, r).start()
            return c
        block_bytes_wait(ybuf, sem_y, 1 - slot)
        lax.fori_loop(0, bm, last, 0)
        block_bytes_wait(xbuf, sem_x, 1 - slot)
        idx_copy(i + 2).wait()
        block_bytes_wait(ybuf, sem_y, slot)


def _experts(block_e, row_slot, h_all, w_gate_up, b_gate_up, w_down, b_down, n_tok, bm):
    n_blocks = row_slot.shape[0]
    plane = n_tok + bm // TOP_K
    assert plane % 8 == 0
    rows = jnp.stack([row_slot >> 2, (row_slot & (TOP_K - 1)) * plane + (row_slot >> 2)], axis=1)
    wmap = lambda i, be: (be[i], 0, 0)
    ys = pl.pallas_call(
        functools.partial(_expert_kernel, bm=bm, n_blocks=n_blocks, n_tok=n_tok, plane=plane),
        grid_spec=pltpu.PrefetchScalarGridSpec(
            num_scalar_prefetch=1,
            grid=(n_blocks,),
            in_specs=[pl.BlockSpec(memory_space=pl.ANY), pl.BlockSpec(memory_space=pl.ANY),
                      pl.BlockSpec((1, D_MODEL, 2 * D_FF), wmap), pl.BlockSpec((1, 1, 2 * D_FF), wmap),
                      pl.BlockSpec((1, D_FF, D_MODEL), wmap), pl.BlockSpec((1, 1, D_MODEL), wmap)],
            out_specs=pl.BlockSpec(memory_space=pl.ANY),
            scratch_shapes=[pltpu.VMEM((2, bm, D_MODEL), F32), pltpu.VMEM((2, bm, D_MODEL), F32),
                            pltpu.SMEM((3, 2, bm), I32),
                            pltpu.VMEM((D_MODEL, 2 * D_FF), BF16), pltpu.VMEM((D_FF, D_MODEL), BF16),
                            pltpu.SemaphoreType.DMA((3,)), pltpu.SemaphoreType.DMA((2,)),
                            pltpu.SemaphoreType.DMA((2,))],
        ),
        out_shape=jax.ShapeDtypeStruct((TOP_K * plane, D_MODEL), F32),
        compiler_params=_cparams(("arbitrary",), manual_dma=True),
        name="experts",
    )(block_e, rows, h_all, w_gate_up, b_gate_up.reshape(N_EXPERTS, 1, 2 * D_FF), w_down,
      b_down.reshape(N_EXPERTS, 1, D_MODEL))
    return ys.reshape(TOP_K, plane, D_MODEL)


def _combine_kernel(ys_ref, route_ref, h_ref, g_ref, b_ref, outp_ref, outs_ref, *, n_prompt_tiles):
    i = pl.program_id(0)
    route = route_ref[...]
    y = route[:, 0:1] * ys_ref[0]
    for r in range(1, TOP_K):
        y = y + route[:, r:r + 1] * ys_ref[r]
    out = _layer_norm(DEEPNORM_ALPHA * h_ref[...] + y, g_ref[...], b_ref[...])

    @pl.when(i < n_prompt_tiles)
    def _():
        outp_ref[...] = out

    @pl.when(i >= n_prompt_tiles)
    def _():
        outs_ref[...] = out


def _combine(ys, route_all, h_all, ln_g, ln_b, n_all, n_prompt, tm=COMBINE_TM):
    n_tiles = n_all // tm
    n_prompt_tiles = n_prompt // tm
    const = lambda a: pl.BlockSpec(a.shape, lambda i: (0,) * a.ndim)
    return pl.pallas_call(
        functools.partial(_combine_kernel, n_prompt_tiles=n_prompt_tiles),
        grid=(n_tiles,),
        in_specs=[pl.BlockSpec((TOP_K, tm, D_MODEL), lambda i: (0, i, 0)), pl.BlockSpec((tm, LANES), lambda i: (i, 0)),
                  pl.BlockSpec((tm, D_MODEL), lambda i: (i, 0)), const(ln_g), const(ln_b)],
        out_specs=[pl.BlockSpec((tm, D_MODEL), lambda i: (jnp.minimum(i, n_prompt_tiles - 1), 0)),
                   pl.BlockSpec((tm, D_MODEL), lambda i: (jnp.maximum(i - n_prompt_tiles, 0), 0))],
        out_shape=[jax.ShapeDtypeStruct((n_prompt, D_MODEL), F32),
                   jax.ShapeDtypeStruct((n_all - n_prompt, D_MODEL), F32)],
        compiler_params=_cparams(("arbitrary",)),
        name="combine",
    )(ys, route_all, h_all, ln_g, ln_b)


def _moe(h_all, route_all, w_gate_up, b_gate_up, w_down, b_down, ln_g, ln_b, n_all, n_prompt, bm=MOE_BM):
    assert h_all.shape[0] >= n_all + bm // TOP_K and bm % TOP_K == 0
    n_slot = n_all * TOP_K
    flat_e = route_all[:n_all, TOP_K:2 * TOP_K].astype(I32).reshape(-1)
    order = jnp.argsort(flat_e, stable=True).astype(I32)
    counts = jnp.sum((flat_e[:, None] == jnp.arange(N_EXPERTS, dtype=I32)[None, :]).astype(I32), axis=0)
    padded = (counts + bm - 1) // bm * bm
    pad_end = jnp.cumsum(padded)
    pad_start = pad_end - padded
    start = jnp.cumsum(counts) - counts
    n_blocks = -(-(n_slot + N_EXPERTS * (bm - 1)) // bm)
    blk_row0 = jnp.arange(n_blocks, dtype=I32) * bm
    block_e = jnp.minimum(jnp.sum((pad_end[None, :] <= blk_row0[:, None]).astype(I32), axis=1), N_EXPERTS - 1)
    k = (blk_row0 - pad_start[block_e])[:, None] + jnp.arange(bm, dtype=I32)[None, :]
    valid = k < counts[block_e][:, None]
    src = jnp.clip(start[block_e][:, None] + k, 0, n_slot - 1)
    row_slot = jnp.where(valid, order[src], n_slot + jnp.arange(bm, dtype=I32)[None, :])
    ys = _experts(block_e, row_slot, h_all, w_gate_up, b_gate_up, w_down, b_down, n_all, bm)
    return _combine(ys, route_all, h_all, ln_g, ln_b, n_all, n_prompt)


def _cmp_sample_kernel(q8_ref, kvc_ref, ocmp_ref, idx_ref, *, sb, pos):
    blk = _lane((8, LANES))
    vis = blk * NSA_BLOCK + (NSA_BLOCK - 1) <= pos
    imps = []
    for s in range(sb):
        kc = kvc_ref[s, :, 0:128].astype(BF16)
        vc = kvc_ref[s, :, 128:256].astype(BF16)
        sc = jnp.where(vis, _dot_nt(q8_ref[s], kc), -jnp.inf)
        mx = jnp.max(sc, axis=1, keepdims=True)
        mx = jnp.where(mx > -jnp.inf, mx, 0.0)
        e = jnp.exp(sc - mx)
        p = e / jnp.maximum(jnp.sum(e, axis=1, keepdims=True), 1e-30)
        ocmp_ref[s] = _dot(p.astype(BF16), vc)
        for kvh in range(NSA_KV_HEADS):
            imps.append(jnp.sum(p[kvh * NSA_GROUP:(kvh + 1) * NSA_GROUP], axis=0, keepdims=True))
    imp = jnp.concatenate(imps, axis=0)
    blk2 = _lane((2 * sb, LANES))
    _, idx = _select_topk(imp, blk2 < (pos // NSA_BLOCK), blk2)
    idx_ref[0] = idx.astype(I32)


def _cmp_sample(q8c, kvc, pos, sb=SAMPLE_SB):
    n = q8c.shape[0]
    return pl.pallas_call(
        functools.partial(_cmp_sample_kernel, sb=sb, pos=pos),
        grid=(n // sb,),
        in_specs=[pl.BlockSpec((sb, 8, LANES), lambda i: (i, 0, 0)), pl.BlockSpec((sb, LANES, 256), lambda i: (i, 0, 0))],
        out_specs=[pl.BlockSpec((sb, 8, LANES), lambda i: (i, 0, 0)), pl.BlockSpec((1, 2 * sb, LANES), lambda i: (i, 0, 0))],
        out_shape=[jax.ShapeDtypeStruct((n, 8, LANES), F32), jax.ShapeDtypeStruct((n // sb, 2 * sb, LANES), I32)],
        compiler_params=_cparams(("parallel",)),
        name="cmp_sample",
    )(q8c, kvc)


def _attend_rows(s, v_t, s_self, v_self):
    mx = jnp.maximum(jnp.max(s, axis=1, keepdims=True), s_self)
    e = jnp.exp(s - mx)
    e_self = jnp.exp(s_self - mx)
    den = jnp.sum(e, axis=1, keepdims=True) + e_self
    num = _dot_nt(e.astype(BF16), v_t) + e_self.astype(BF16).astype(F32) * v_self
    return num / den


def _sample_kernel(pt_ref, sel_ref, q8r_ref, qsb_ref, qmem_ref, nsa_new_ref, win_new_ref, gate_ref, ocmp_ref,
                   cwin_ref, cmem_ref, nsa_hbm, sb_hbm, onsa_ref, osb_ref, omem_ref,
                   slc_buf, sb_buf, sem_slc, sem_sb, *, n_pages):
    i = pl.program_id(0)
    n_sel = NSA_KV_HEADS * NSA_TOPK

    def sel_block(j):
        return sel_ref[i * 2 * 16 + (j // NSA_TOPK) * 16 + j % NSA_TOPK]

    def slc_copy(j):
        page = pt_ref[i * n_pages + sel_block(j) // 2]
        return pltpu.make_async_copy(nsa_hbm.at[page, 256:512, :],
                                     slc_buf.at[j // NSA_TOPK, :, pl.ds((j % NSA_TOPK) * PAGE_SIZE, PAGE_SIZE)],
                                     sem_slc.at[0])

    def sb_copy(p):
        return pltpu.make_async_copy(sb_hbm.at[pt_ref[i * n_pages + p]], sb_buf.at[p % 2], sem_sb.at[p % 2])

    def issue(j, c):
        slc_copy(j).start()
        return c
    lax.fori_loop(0, n_sel, issue, 0)
    sb_copy(n_pages - 1).start()

    q8r = q8r_ref[0]
    row8 = _row((8, LANES))
    lane8 = _lane((8, LANES))
    head_half = (lane8 < HEAD_DIM) == (row8 < NSA_GROUP)

    kw = cwin_ref[0, 0:128, :].astype(BF16)
    vw = cwin_ref[0, 128:256, :].astype(BF16)
    k_new = win_new_ref[0, :, 0:128].astype(BF16).astype(F32)
    v_new = win_new_ref[0, :, 128:256].astype(BF16).astype(F32)
    s = jnp.where(_lane((8, NSA_WINDOW)) >= 1, _dot(q8r, kw), NEG)
    s_self = jnp.sum(q8r.astype(F32) * k_new, axis=1, keepdims=True)
    o_win = _attend_rows(s, vw, s_self, v_new)

    qm = qmem_ref[0]
    sm = _dot(qm, cmem_ref[0, 0:256, :].astype(BF16))
    em = jnp.exp(sm - jnp.max(sm, axis=1, keepdims=True))
    pm = em / jnp.sum(em, axis=1, keepdims=True)
    omem_ref[0] = _dot_nt(pm.astype(BF16), cmem_ref[0, 256:512, :].astype(BF16))

    def wait_slc(j, c):
        slc_copy(j).wait()
        return c
    lax.fori_loop(0, n_sel, wait_slc, 0)
    k_new = nsa_new_ref[0, :, 256:384].astype(BF16).astype(F32)
    v_new = nsa_new_ref[0, :, 384:512].astype(BF16).astype(F32)
    s_self = jnp.sum(q8r.astype(F32) * k_new, axis=1, keepdims=True)
    o_k = []
    for kvh in range(NSA_KV_HEADS):
        in_block = jnp.concatenate(
            [(lane8 // NSA_BLOCK) == (sel_block(kvh * NSA_TOPK + j) % 2) for j in range(NSA_TOPK)], axis=1)
        sc = jnp.where(in_block, _dot(q8r, slc_buf[kvh, 0:128, :].astype(BF16)), NEG)
        o_k.append(_attend_rows(sc, slc_buf[kvh, 128:256, :].astype(BF16), s_self, v_new))
    o_slc = jnp.where(row8 < NSA_GROUP, o_k[0], o_k[1])
    gate = gate_ref[0]
    merged = gate[:, 0:1] * ocmp_ref[0] + gate[:, 1:2] * o_slc + gate[:, 2:3] * o_win
    onsa_ref[0] = jnp.where(head_half, merged, 0.0)

    qs = qsb_ref[0]
    u2 = _suffix_matrix(PAGE_SIZE)
    all_keys = jnp.full((8, PAGE_SIZE), True)

    def cond(state):
        p, top, _, _ = state
        return (p >= 0) & (top > SB_EXIT)

    def body(state):
        p, _, carry, acc = state
        sb_copy(p).wait()

        @pl.when(p > 0)
        def _():
            sb_copy(p - 1).start()
        a, carry = _sb_tile(_dot(qs, sb_buf[p % 2, 0:256, :].astype(BF16)), all_keys, carry, u2)
        contrib = _dot_nt(a, sb_buf[p % 2, 256:512, :].astype(BF16))
        return p - 1, jnp.max(carry[0:SB_HEADS]), carry, acc + contrib

    p_end, _, _, acc = lax.while_loop(
        cond, body, (n_pages - 1, jnp.zeros((), F32), jnp.zeros((8, LANES), F32), jnp.zeros((8, 256), F32)))

    @pl.when(p_end >= 0)
    def _():
        sb_copy(p_end).wait()
    osb_ref[0] = acc


def _sample_attend(page_table, sel, q8r, qsb8, qmem8, nsa_new, win_new, gate8, ocmp8, cache_win, cache_mem,
                   cache_nsa, cache_sb):
    n = q8r.shape[0]
    n_pages = page_table.shape[0] // n
    per = lambda a: pl.BlockSpec((1,) + a.shape[1:], lambda i, pt, sl: (i,) + (0,) * (a.ndim - 1))
    vm = [q8r, qsb8, qmem8, nsa_new, win_new, gate8, ocmp8, cache_win, cache_mem]
    return pl.pallas_call(
        functools.partial(_sample_kernel, n_pages=n_pages),
        grid_spec=pltpu.PrefetchScalarGridSpec(
            num_scalar_prefetch=2,
            grid=(n,),
            in_specs=[per(a) for a in vm] + [pl.BlockSpec(memory_space=pl.ANY)] * 2,
            out_specs=[pl.BlockSpec((1, 8, LANES), lambda i, pt, sl: (i, 0, 0)),
                       pl.BlockSpec((1, 8, 256), lambda i, pt, sl: (i, 0, 0)),
                       pl.BlockSpec((1, 8, 256), lambda i, pt, sl: (i, 0, 0))],
            scratch_shapes=[pltpu.VMEM((NSA_KV_HEADS, 256, NSA_TOPK * PAGE_SIZE), F32),
                            pltpu.VMEM((2, 512, PAGE_SIZE), F32),
                            pltpu.SemaphoreType.DMA((1,)), pltpu.SemaphoreType.DMA((2,))],
        ),
        out_shape=[jax.ShapeDtypeStruct((n, 8, LANES), F32), jax.ShapeDtypeStruct((n, 8, 256), F32),
                   jax.ShapeDtypeStruct((n, 8, 256), F32)],
        compiler_params=_cparams(("arbitrary",)),
        name="sample_attend",
    )(page_table, sel, *vm, cache_nsa, cache_sb)


def _prep_mixer_weights(w_in, pe_cmp, w_cmp1, b_cmp1, w_cmp2, b_cmp2, w_out):
    head_perm = np.array([kvh * NSA_GROUP + g for g in range(NSA_GROUP) for kvh in range(NSA_KV_HEADS)])
    qcols = (head_perm[:, None] * HEAD_DIM + np.arange(HEAD_DIM)[None, :]).reshape(-1)
    w_q = w_in[:, 0:512][:, qcols]
    w_kv = w_in[:, 512:1280]
    w_g = jnp.pad(w_in[:, 1280:1304], ((0, 0), (0, LANES - 24)))
    w_all = jnp.concatenate([w_q, w_kv, w_in[:, 1304:2328], w_g], axis=1).astype(BF16)
    w_out_p = jnp.concatenate([w_out[0:512][qcols], w_out[512:]], axis=0).astype(BF16)
    w1 = jnp.transpose(w_cmp1.reshape(2, NSA_BLOCK, HEAD_DIM, CMP_HIDDEN), (0, 2, 1, 3))
    w1bd = jnp.zeros((2, HEAD_DIM, 2, NSA_BLOCK, 2, CMP_HIDDEN), F32)
    w2bd = jnp.zeros((2, 2, CMP_HIDDEN, 2, HEAD_DIM), F32)
    for blk in range(2):
        w1bd = w1bd.at[:, :, blk, :, blk, :].set(w1)
        w2bd = w2bd.at[:, blk, :, blk, :].set(w_cmp2)
    n_groups = HEAD_DIM // COMPRESS_DG
    w1bd = w1bd.reshape(2, n_groups, COMPRESS_DG * LANES, 2 * CMP_HIDDEN).astype(BF16)
    w2bd = w2bd.reshape(2, 2 * CMP_HIDDEN, LANES).astype(BF16)
    pe_t = jnp.tile(jnp.transpose(pe_cmp, (0, 2, 1)), (1, 1, 2)).reshape(2, n_groups, 1, COMPRESS_DG * LANES)
    b1t = jnp.tile(b_cmp1, (1, 2)).reshape(2, 1, 2 * CMP_HIDDEN)
    b2t = jnp.tile(b_cmp2, (1, 2)).reshape(2, 1, LANES)
    return w_all, w_out_p, (pe_t, w1bd, b1t, w2bd, b2t)


def _rope_tables(pos):
    half = HEAD_DIM // 2
    inv = ROPE_THETA ** (-jnp.arange(half, dtype=F32) / half)
    ang = pos.astype(F32)[:, None] * inv[None, :]
    cos, sin = jnp.cos(ang), jnp.sin(ang)
    cos_t = jnp.concatenate([cos, cos, cos, cos], axis=1)
    sin_t = jnp.concatenate([-sin, sin, -sin, sin], axis=1)
    return cos_t, sin_t


def _head_rows(q, n_heads, width):
    n = q.shape[0]
    lane_head = (np.arange(width) // HEAD_DIM)[None, None, :]
    rows = jnp.where(lane_head == np.arange(n_heads)[None, :, None], q[:, None, :], jnp.zeros((), q.dtype))
    return jnp.pad(rows, ((0, 0), (0, 8 - n_heads), (0, 0)))


def _nsa_head_rows(q):
    n = q.shape[0]
    q4 = q.reshape(n, NSA_GROUP, LANES)
    low = (np.arange(LANES) < HEAD_DIM)[None, None, :]
    zero = jnp.zeros((), q.dtype)
    return jnp.concatenate([jnp.where(low, q4, zero), jnp.where(low, zero, q4)], axis=1)


def _diag_heads(o, n_heads):
    lane_head = (np.arange(o.shape[2]) // HEAD_DIM)[None, None, :]
    keep = lane_head == np.arange(8)[None, :, None]
    return jnp.sum(jnp.where(keep, o, 0.0), axis=1)


def kernel(x_prompt, x_sample, mem_prompt, cache_nsa, cache_sb, cache_win, cache_mem, page_table, w_in, pe_cmp, w_cmp1,
           b_cmp1, w_cmp2, b_cmp2, w_mem_kv, w_out, ln1_g, ln1_b, w_router, b_router, w_gate_up, b_gate_up, w_down,
           b_down, ln2_g, ln2_b):
    assert w_in.shape[0] == DEPTH
    b, t, d = x_prompt.shape
    db = x_sample.shape[0]
    n = b * t
    n_pages = page_table.shape[1]
    n_phys = cache_nsa.shape[1]
    past = n_pages * PAGE_SIZE
    w_all, w_out_p, cw = _prep_mixer_weights(w_in[0], pe_cmp[0], w_cmp1[0], b_cmp1[0], w_cmp2[0], b_cmp2[0], w_out[0])
    ln1 = (ln1_g[0].reshape(1, d), ln1_b[0].reshape(1, d))
    ln2 = (ln2_g[0].reshape(1, d), ln2_b[0].reshape(1, d))
    wr = jnp.pad(w_router[0], ((0, 0), (0, LANES - N_EXPERTS)))
    br = jnp.concatenate([b_router[0], jnp.full((LANES - N_EXPERTS,), NEG, F32)]).reshape(1, LANES)

    cos_t, sin_t = _rope_tables(jnp.arange(t, dtype=I32))
    qc, qr, nsa_t, win_t, sb_t, slc_b, win_b, sb_b, qsb, qmem, gate = _project(x_prompt.reshape(n, d), w_all, cos_t,
                                                                               sin_t, b, t)
    kvc = _blocks_from_compress(_compress(nsa_t, jnp.arange(n // PAGE_SIZE, dtype=I32), *cw), b)
    ocmp, bias = _cmp_attend(qc.reshape(b, t, 512), kvc)
    onsa = _nsa_attend(qr.reshape(b, t, 512), bias, ocmp, gate, slc_b.reshape(b, t, 256), win_b.reshape(b, t, 256))
    osb = _sb_attend(qsb.reshape(b, t, 256), sb_b.reshape(b, t, 512))
    mkv, mkv_t = _memkv(mem_prompt, w_mem_kv[0].astype(BF16))
    omem = _mem_attend(qmem.reshape(b, t, 256), mkv)

    feature_major = lambda c: jnp.transpose(c[0], (0, 2, 3, 4, 1)).reshape(c.shape[1], -1, c.shape[2])
    cos_s, sin_s = _rope_tables(jnp.full((db,), past, I32))
    qc_s, qr_s, nsa_ts, win_ts, sb_ts, _, _, _, qsb_s, qmem_s, gate_s = _project(x_sample.reshape(db, d), w_all, cos_s,
                                                                               sin_s, 1, db)
    nsa_s, win_s = nsa_ts[0].T, win_ts[0].T
    pages_nsa = feature_major(cache_nsa)
    pages_sb = feature_major(cache_sb)
    pt_flat = page_table.reshape(-1)
    kvc_s = _blocks_from_compress(_compress(pages_nsa, pt_flat, *cw), db)
    ocmp8, idx = _cmp_sample(_nsa_head_rows(qc_s), kvc_s, past)
    sel = idx[:, :, :16].reshape(-1)
    gate8 = jnp.pad(gate_s[:, :24].reshape(db, 8, 3), ((0, 0), (0, 0), (0, LANES - 3)))
    onsa8, osb8, omem8 = _sample_attend(
        pt_flat, sel, _nsa_head_rows(qr_s), _head_rows(qsb_s, SB_HEADS, 256), _head_rows(qmem_s, MEM_HEADS, 256),
        nsa_s.reshape(db, 1, 512), win_s.reshape(db, 1, 256), gate8, ocmp8,
        feature_major(cache_win), feature_major(cache_mem), pages_nsa, pages_sb)
    onsa_s = (onsa8[:, :NSA_GROUP] + onsa8[:, NSA_GROUP:]).reshape(db, 512)
    tail = _outproj(onsa_s, _diag_heads(osb8, SB_HEADS), _diag_heads(omem8, MEM_HEADS), x_sample.reshape(db, d),
                    w_out_p, *ln1, wr, br)

    h_all, route_all = _outproj(onsa, osb, omem, x_prompt.reshape(n, d), w_out_p, *ln1, wr, br, tail=tail)
    yp, ys = _moe(h_all, route_all, w_gate_up[0], b_gate_up[0], w_down[0], b_down[0], *ln2, n + db, n)

    def rows(x_t, kinds, heads):
        s, _, p = x_t.shape
        return jnp.transpose(x_t.reshape(s, kinds, heads, HEAD_DIM, p), (0, 4, 1, 2, 3))[None]

    win_len = min(NSA_WINDOW, t)
    win_all_t = jnp.concatenate([feature_major(cache_win), win_s[:, :, None]], axis=2)[:, :, 1:]
    return (yp.reshape(b, t, d), ys.reshape(db, 1, d),
            rows(nsa_t, 4, NSA_KV_HEADS), rows(win_t[:, :, t - win_len:], 2, NSA_KV_HEADS),
            rows(sb_t, 2, SB_HEADS), rows(mkv_t, 2, MEM_HEADS),
            jnp.transpose(rows(nsa_ts, 4, NSA_KV_HEADS), (0, 2, 1, 3, 4, 5)), rows(win_all_t, 2, NSA_KV_HEADS),
            jnp.transpose(rows(sb_ts, 2, SB_HEADS), (0, 2, 1, 3, 4, 5)))
```

```python
import functools

import jax
import jax.numpy as jnp
import numpy as np
from jax import lax
from jax.experimental import pallas as pl
from jax.experimental.pallas import tpu as pltpu

F32 = jnp.float32
BF16 = jnp.bfloat16
I32 = jnp.int32

D_MODEL = 1024
HEAD_DIM = 64
NSA_HEADS = 8
NSA_KV_HEADS = 2
NSA_GROUP = 4
SB_HEADS = 4
MEM_HEADS = 4
N_MEM = 256
NSA_BLOCK = 64
NSA_TOPK = 15
NSA_WINDOW = 512
CMP_HIDDEN = 128
PAGE_SIZE = 128
ROPE_THETA = 10000.0
N_EXPERTS = 32
TOP_K = 4
D_FF = 1024
SWIGLU_LIMIT = 7.0
SWIGLU_ALPHA = 1.702
LN_EPS = 1e-5
DEPTH = 1
DEEPNORM_ALPHA = (2.0 * DEPTH) ** 0.25
SCALE = HEAD_DIM ** -0.5

LANES = 128
NEG = -1e30
M_INIT = -1e20
SB_EXIT = -100.0
VMEM_LIMIT = 56 * 1024 * 1024

PROJ_TM = 256
COMPRESS_PG = 64
COMPRESS_DG = 8
CMP_TQ = 256
NSA_TQ = 256
NSA_TK = 512
SB_TQ = 256
MEM_TQ = 256
OUT_TM = 256
MOE_BM = 256
COMBINE_TM = 128
SAMPLE_SB = 8

C_Q, C_KV, C_QSB, C_KSB, C_VSB, C_QMEM, C_G, C_END = 0, 512, 1280, 1536, 1792, 2048, 2304, 2432


def _cparams(sem, manual_dma=False):
    return pltpu.CompilerParams(dimension_semantics=sem, vmem_limit_bytes=VMEM_LIMIT,
                                disable_bounds_checks=manual_dma)


def _dot_nt(a, b):
    return lax.dot_general(a, b, (((1,), (1,)), ((), ())), preferred_element_type=F32)


def _dot(a, b):
    return jnp.dot(a, b, preferred_element_type=F32)


def _lane(shape):
    return lax.broadcasted_iota(I32, shape, 1)


def _row(shape):
    return lax.broadcasted_iota(I32, shape, 0)


def _proj_kernel(x_ref, w_ref, cos_ref, sin_ref, qc_ref, qr_ref, nsa_ref, win_ref, sb_ref,
                 slc_b_ref, win_b_ref, sb_b_ref, qsb_ref, qmem_ref, gate_ref):
    xb = x_ref[...].astype(BF16)
    cos = cos_ref[...]
    sin = sin_ref[...]
    tm = xb.shape[0]
    first_half = (_lane((tm, LANES)) % HEAD_DIM) < (HEAD_DIM // 2)

    def mm(lo, hi):
        return _dot(xb, w_ref[:, lo:hi])

    def rope(v):
        partner = jnp.where(first_half, pltpu.roll(v, LANES - HEAD_DIM // 2, 1), pltpu.roll(v, HEAD_DIM // 2, 1))
        return v * cos + partner * sin

    q = mm(C_Q, C_KV)
    for c in range(4):
        ch = q[:, c * LANES:(c + 1) * LANES]
        qc_ref[:, c * LANES:(c + 1) * LANES] = (ch * SCALE).astype(BF16)
        qr_ref[:, c * LANES:(c + 1) * LANES] = (rope(ch) * SCALE).astype(BF16)
    kv = mm(C_KV, C_QSB)
    nsa_ref[0, 0:256, :] = kv[:, 0:256].T
    k_slc = rope(kv[:, 256:384])
    v_slc = kv[:, 384:512]
    nsa_ref[0, 256:384, :] = k_slc.T
    nsa_ref[0, 384:512, :] = v_slc.T
    slc_b_ref[:, 0:128] = k_slc.astype(BF16)
    slc_b_ref[:, 128:256] = v_slc.astype(BF16)
    k_win = rope(kv[:, 512:640])
    v_win = kv[:, 640:768]
    win_ref[0, 0:128, :] = k_win.T
    win_ref[0, 128:256, :] = v_win.T
    win_b_ref[:, 0:128] = k_win.astype(BF16)
    win_b_ref[:, 128:256] = v_win.astype(BF16)
    qsb_ref[...] = (mm(C_QSB, C_KSB) * SCALE).astype(BF16)
    sb = mm(C_KSB, C_QMEM)
    sb_ref[0] = sb.T
    sb_b_ref[...] = sb.astype(BF16)
    qmem_ref[...] = (mm(C_QMEM, C_G) * SCALE).astype(BF16)
    gate_ref[...] = jax.nn.sigmoid(mm(C_G, C_END))


def _project(x, w_all, cos_t, sin_t, b, t, tm=PROJ_TM):
    n = x.shape[0]
    tm = min(tm, t)
    nt = t // tm
    row = lambda w: pl.BlockSpec((tm, w), lambda i: (i, 0))
    tab = pl.BlockSpec((tm, LANES), lambda i: (i % nt, 0))
    outs = [(512, BF16, False), (512, BF16, False), (512, F32, True), (256, F32, True), (512, F32, True),
            (256, BF16, False), (256, BF16, False), (512, BF16, False), (256, BF16, False), (256, BF16, False),
            (128, F32, False)]
    spec = lambda w, fm: pl.BlockSpec((1, w, tm), lambda i: (i // nt, 0, i % nt)) if fm else row(w)
    shape = lambda w, dt, fm: jax.ShapeDtypeStruct((b, w, t) if fm else (n, w), dt)
    return pl.pallas_call(
        _proj_kernel,
        grid=(n // tm,),
        in_specs=[row(D_MODEL), pl.BlockSpec((D_MODEL, C_END), lambda i: (0, 0)), tab, tab],
        out_specs=[spec(w, fm) for w, _, fm in outs],
        out_shape=[shape(*o) for o in outs],
        compiler_params=_cparams(("parallel",)),
        name="proj",
    )(x, w_all, cos_t, sin_t)


def _compress_kernel(pt_ref, rows_hbm, pe_ref, w1_ref, b1_ref, w2_ref, b2_ref, out_ref, buf, sem, *, pg, n_steps, ppb):
    i = pl.program_id(0)
    m = 2 * pg
    slabs = [(kind, head) for kind in range(2) for head in range(NSA_KV_HEADS)]

    def slab_copy(step, slot, j, kind, head):
        page = pt_ref[step * pg + j]
        src = rows_hbm.at[page // ppb, pl.ds(kind * LANES + head * HEAD_DIM, HEAD_DIM),
                          pl.ds((page % ppb) * PAGE_SIZE, PAGE_SIZE)]
        return pltpu.make_async_copy(src, buf.at[slot, kind, :, NSA_KV_HEADS * j + head, :], sem.at[slot])

    def issue(step, slot):
        def body(j, c):
            for kind, head in slabs:
                slab_copy(step, slot, j, kind, head).start()
            return c
        lax.fori_loop(0, pg, body, 0)

    @pl.when(i == 0)
    def _():
        issue(0, 0)

    @pl.when(i + 1 < n_steps)
    def _():
        issue(i + 1, (i + 1) % 2)

    slot = i % 2

    def wait_body(j, c):
        for kind, head in slabs:
            slab_copy(i, slot, j, kind, head).wait()
        return c
    lax.fori_loop(0, pg, wait_body, 0)

    n_groups = w1_ref.shape[1]
    dg = HEAD_DIM // n_groups
    for kind in range(2):
        acc = b1_ref[kind]
        for g in range(n_groups):
            xk = jnp.concatenate([buf[slot, kind, g * dg + dd] for dd in range(dg)], axis=1)
            acc = acc + _dot((xk + pe_ref[kind, g]).astype(BF16), w1_ref[kind, g])
        h = jax.nn.gelu(acc)
        out_ref[kind] = _dot(h.astype(BF16), w2_ref[kind]) + b2_ref[kind]


def _compress(rows3, page_table, pe_t, w1bd, b1t, w2bd, b2t, pg=COMPRESS_PG):
    n_pages = page_table.shape[0]
    n_steps = n_pages // pg
    m = 2 * pg
    const = lambda a: pl.BlockSpec(a.shape, lambda i, pt: (0,) * a.ndim)
    return pl.pallas_call(
        functools.partial(_compress_kernel, pg=pg, n_steps=n_steps, ppb=rows3.shape[2] // PAGE_SIZE),
        grid_spec=pltpu.PrefetchScalarGridSpec(
            num_scalar_prefetch=1,
            grid=(n_steps,),
            in_specs=[pl.BlockSpec(memory_space=pl.ANY), const(pe_t), const(w1bd), const(b1t), const(w2bd), const(b2t)],
            out_specs=pl.BlockSpec((2, m, LANES), lambda i, pt: (0, i, 0)),
            scratch_shapes=[pltpu.VMEM((2, 2, HEAD_DIM, m, LANES), F32), pltpu.SemaphoreType.DMA((2,))],
        ),
        out_shape=jax.ShapeDtypeStruct((2, 2 * n_pages, LANES), F32),
        compiler_params=_cparams(("arbitrary",)),
        name="compress",
    )(page_table, rows3, pe_t, w1bd, b1t, w2bd, b2t)


def _blocks_from_compress(out, n_seq):
    n_pages = out.shape[1] // 2
    o = out.reshape(2, n_pages, NSA_KV_HEADS, 2, HEAD_DIM)
    o = jnp.transpose(o, (1, 3, 0, 2, 4))
    return o.reshape(n_seq, 2 * n_pages // n_seq, 256)


def _select_topk(imp, cand, blk):
    tq = imp.shape[0]
    blk_f = blk.astype(F32)
    lane = _lane((tq, LANES))
    impm = jnp.where(cand, imp, -1.0)
    sel = jnp.zeros((tq, LANES), F32)
    idx = jnp.zeros((tq, LANES), F32)
    for r in range(NSA_TOPK):
        mx = jnp.max(impm, axis=1, keepdims=True)
        am = jnp.min(jnp.where(impm == mx, blk_f, float(LANES)), axis=1, keepdims=True)
        hit = blk_f == am
        sel = jnp.where(hit & (mx >= 0.0), 1.0, sel)
        idx = jnp.where(lane == r, am, idx)
        impm = jnp.where(hit, -2.0, impm)
    return sel, idx


def _cmp_kernel(qc_ref, kvc_ref, ocmp_ref, bias_ref, *, tq):
    t0 = pl.program_id(1) * tq
    pos = t0 + _row((tq, LANES))
    blk = _lane((tq, LANES))
    low = blk < HEAD_DIM
    vis = blk * NSA_BLOCK + (NSA_BLOCK - 1) <= pos
    cur = pos >> 6
    cand = blk < cur
    kc = kvc_ref[0, :, 0:128].astype(BF16)
    vc = kvc_ref[0, :, 128:256].astype(BF16)
    imp = [jnp.zeros((tq, LANES), F32), jnp.zeros((tq, LANES), F32)]
    for g in range(NSA_GROUP):
        qch = qc_ref[0, :, g * LANES:(g + 1) * LANES]
        o = []
        for kvh in range(NSA_KV_HEADS):
            qm = jnp.where(low if kvh == 0 else ~low, qch, jnp.zeros_like(qch))
            s = jnp.where(vis, _dot_nt(qm, kc), -jnp.inf)
            mx = jnp.max(s, axis=1, keepdims=True)
            mx = jnp.where(mx > -jnp.inf, mx, 0.0)
            e = jnp.exp(s - mx)
            p = e / jnp.maximum(jnp.sum(e, axis=1, keepdims=True), 1e-30)
            imp[kvh] = imp[kvh] + p
            o.append(_dot(p.astype(BF16), vc))
        ocmp_ref[0, :, g * LANES:(g + 1) * LANES] = jnp.where(low, o[0], o[1])
    for kvh in range(NSA_KV_HEADS):
        sel, _ = _select_topk(imp[kvh], cand, blk)
        keep = (sel > 0.5) | (blk == cur)
        bias_ref[0, :, kvh * LANES:(kvh + 1) * LANES] = jnp.where(keep, 0.0, NEG).astype(BF16)


def _cmp_attend(qc, kvc, tq=CMP_TQ):
    b, t, _ = qc.shape
    tq = min(tq, t)
    return pl.pallas_call(
        functools.partial(_cmp_kernel, tq=tq),
        grid=(b, t // tq),
        in_specs=[pl.BlockSpec((1, tq, 512), lambda i, j: (i, j, 0)),
                  pl.BlockSpec((1, LANES, 256), lambda i, j: (i, 0, 0))],
        out_specs=[pl.BlockSpec((1, tq, 512), lambda i, j: (i, j, 0)),
                   pl.BlockSpec((1, tq, 256), lambda i, j: (i, j, 0))],
        out_shape=[jax.ShapeDtypeStruct((b, t, 512), F32), jax.ShapeDtypeStruct((b, t, 256), BF16)],
        compiler_params=_cparams(("parallel", "parallel")),
        name="cmp_attend",
    )(qc, kvc)


def _rep(x, k):
    return x if k == 1 else jnp.concatenate([x] * k, axis=1)


def _softmax_step(s, v_aug, m_ref, acc_ref):
    k = s.shape[1] // LANES
    m_prev = m_ref[...]
    m_new = jnp.maximum(m_prev, jnp.max(s, axis=1, keepdims=True))
    alpha = jnp.exp(m_prev - m_new)
    p = jnp.exp(s - _rep(m_new, k)).astype(BF16)
    acc_ref[...] = _rep(alpha, 2) * acc_ref[...] + _dot(p, v_aug)
    m_ref[...] = m_new


def _nsa_kernel(qr_ref, bias_ref, ocmp_ref, gate_ref, slc_ref, win_ref, out_ref, m_ref, acc_ref, *, tq, tk, band):
    j = pl.program_id(1)
    t = slc_ref.shape[1]
    t0 = j * tq
    rows = NSA_GROUP * tq
    low_q = _lane((tq, LANES)) < HEAD_DIM
    qpos = t0 + _row((tq, LANES))
    qpos_rows = jnp.concatenate([qpos] * NSA_GROUP, axis=0)
    heads = range(NSA_KV_HEADS)

    qm, qaug = [], []
    for kvh in heads:
        keep = low_q if kvh == 0 else ~low_q
        qm.append(jnp.concatenate(
            [jnp.where(keep, qr_ref[0, :, g * LANES:(g + 1) * LANES], jnp.zeros((tq, LANES), BF16))
             for g in range(NSA_GROUP)], axis=0))
        bias = bias_ref[0, :, kvh * LANES:(kvh + 1) * LANES]
        qaug.append(jnp.concatenate([qm[kvh], jnp.concatenate([bias] * NSA_GROUP, axis=0)], axis=1))

    m_ref[...] = jnp.full(m_ref.shape, M_INIT, F32)
    acc_ref[...] = jnp.zeros(acc_ref.shape, F32)
    ones_k = jnp.ones((tk, LANES), BF16)

    def slc_tile(kt, causal):
        k0 = pl.multiple_of(kt * tk, tk)
        kblk = (k0 + _row((tk, LANES))) >> 6
        onehot = jnp.where(kblk == _lane((tk, LANES)), 1.0, 0.0).astype(BF16)
        k_aug = jnp.concatenate([slc_ref[0, pl.ds(k0, tk), 0:128], onehot], axis=1)
        v_aug = jnp.concatenate([slc_ref[0, pl.ds(k0, tk), 128:256], ones_k], axis=1)
        for kvh in heads:
            s = _dot_nt(qaug[kvh], k_aug)
            if causal:
                s = jnp.where(k0 + _lane((rows, tk)) <= _rep(qpos_rows, tk // LANES), s, NEG)
            _softmax_step(s, v_aug, m_ref.at[kvh], acc_ref.at[kvh])

    n_open = t0 // tk
    n_tiles = (t0 + tq + tk - 1) // tk

    def open_body(kt, c):
        slc_tile(kt, False)
        return c

    def diag_body(kt, c):
        slc_tile(kt, True)
        return c
    lax.fori_loop(0, n_open, open_body, 0)
    lax.fori_loop(n_open, n_tiles, diag_body, 0)
    o_slc = [acc_ref[kvh, :, 0:LANES] / acc_ref[kvh, :, LANES:2 * LANES] for kvh in heads]

    w0 = pl.multiple_of(jnp.clip(t0 - NSA_WINDOW, 0, t - band), LANES)
    kw = win_ref[0, pl.ds(w0, band), 0:128]
    vw_aug = jnp.concatenate([win_ref[0, pl.ds(w0, band), 128:256], jnp.ones((band, LANES), BF16)], axis=1)
    kpos = w0 + _lane((rows, band))
    qp = _rep(qpos_rows, band // LANES)
    in_window = (kpos <= qp) & (kpos > qp - NSA_WINDOW)
    o_win = []
    for kvh in heads:
        s = jnp.where(in_window, _dot_nt(qm[kvh], kw), NEG)
        p = jnp.exp(s - jnp.max(s, axis=1, keepdims=True)).astype(BF16)
        o = _dot(p, vw_aug)
        o_win.append(o[:, 0:LANES] / o[:, LANES:2 * LANES])

    gate = gate_ref[...]

    def gate_col(g, c):
        a = gate[:, (0 * NSA_GROUP + g) * 3 + c:(0 * NSA_GROUP + g) * 3 + c + 1]
        b = gate[:, (1 * NSA_GROUP + g) * 3 + c:(1 * NSA_GROUP + g) * 3 + c + 1]
        return jnp.where(low_q, a, b)

    for g in range(NSA_GROUP):
        sl = slice(g * tq, (g + 1) * tq)
        slc = jnp.where(low_q, o_slc[0][sl], o_slc[1][sl])
        win = jnp.where(low_q, o_win[0][sl], o_win[1][sl])
        cmp_ = ocmp_ref[0, :, g * LANES:(g + 1) * LANES]
        out_ref[:, g * LANES:(g + 1) * LANES] = gate_col(g, 0) * cmp_ + gate_col(g, 1) * slc + gate_col(g, 2) * win


def _nsa_attend(qr, bias, ocmp, gate, slc_b, win_b, tq=NSA_TQ, tk=NSA_TK):
    b, t, _ = qr.shape
    tq, tk = min(tq, t), min(tk, t)
    nt = t // tq
    rows = NSA_GROUP * tq
    band = min(NSA_WINDOW + tq, t)
    tile = lambda w: pl.BlockSpec((1, tq, w), lambda i, j: (i, j, 0))
    full = lambda w: pl.BlockSpec((1, t, w), lambda i, j: (i, 0, 0))
    flat = lambda w: pl.BlockSpec((tq, w), lambda i, j: (i * nt + j, 0))
    return pl.pallas_call(
        functools.partial(_nsa_kernel, tq=tq, tk=tk, band=band),
        grid=(b, nt),
        in_specs=[tile(512), tile(256), tile(512), flat(LANES), full(256), full(256)],
        out_specs=flat(512),
        out_shape=jax.ShapeDtypeStruct((b * t, 512), F32),
        scratch_shapes=[pltpu.VMEM((NSA_KV_HEADS, rows, LANES), F32), pltpu.VMEM((NSA_KV_HEADS, rows, 2 * LANES), F32)],
        compiler_params=_cparams(("parallel", "arbitrary")),
        name="nsa_attend",
    )(qr, bias, ocmp, gate, slc_b, win_b)


def _log_sigmoid(z):
    return jnp.minimum(z, 0.0) - jnp.log1p(jnp.exp(-jnp.abs(z)))


def _suffix_matrix(tk):
    r = _row((tk, 2 * LANES))
    c = _lane((tk, 2 * LANES))
    return jnp.where((r > c) | (c >= LANES), 1.0, 0.0).astype(BF16)


def _sb_tile(z, mask, carry, u2):
    ls = _log_sigmoid(z)
    lk = jnp.where(mask, ls - z, 0.0)
    hi = lk.astype(BF16)
    lo = (lk - hi.astype(F32)).astype(BF16)
    cs = _dot(hi, u2) + _dot(lo, u2)
    a = jnp.where(mask, jnp.exp(ls + carry + cs[:, :LANES]), 0.0)
    return a.astype(BF16), carry + cs[:, LANES:]


def _sb_kernel(q_ref, kv_ref, out_ref, carry_ref, acc_ref, *, tq):
    j = pl.program_id(1)
    t0 = j * tq
    rows = SB_HEADS * tq
    low = _lane((tq, LANES)) < HEAD_DIM
    qpos_rows = jnp.concatenate([t0 + _row((tq, LANES))] * SB_HEADS, axis=0)
    u2 = _suffix_matrix(LANES)
    qpair = []
    for c in range(SB_HEADS // 2):
        ch = q_ref[0, :, c * LANES:(c + 1) * LANES]
        zero = jnp.zeros_like(ch)
        qpair.append(jnp.concatenate([jnp.where(low, ch, zero), jnp.where(low, zero, ch)], axis=0))
    carry_ref[...] = jnp.zeros_like(carry_ref)
    acc_ref[...] = jnp.zeros_like(acc_ref)

    def cond(state):
        kt, top = state
        return (kt >= 0) & (top > SB_EXIT)

    def body(state):
        kt, _ = state
        k0 = pl.multiple_of(kt * LANES, LANES)
        mask = (k0 + _lane((rows, LANES))) < qpos_rows
        z = jnp.concatenate([_dot_nt(qpair[c], kv_ref[0, pl.ds(k0, LANES), c * LANES:(c + 1) * LANES])
                             for c in range(SB_HEADS // 2)], axis=0)
        a, carry = _sb_tile(z, mask, carry_ref[...], u2)
        contrib = jnp.concatenate(
            [_dot(a[2 * c * tq:2 * (c + 1) * tq], kv_ref[0, pl.ds(k0, LANES), 256 + c * LANES:256 + (c + 1) * LANES])
             for c in range(SB_HEADS // 2)], axis=0)
        acc_ref[...] = acc_ref[...] + contrib
        carry_ref[...] = carry
        return kt - 1, jnp.max(carry)

    lax.while_loop(cond, body, ((t0 + tq) // LANES - 1, jnp.zeros((), F32)))
    for c in range(SB_HEADS // 2):
        out_ref[:, c * LANES:(c + 1) * LANES] = jnp.where(low, acc_ref[2 * c * tq:(2 * c + 1) * tq],
                                                          acc_ref[(2 * c + 1) * tq:(2 * c + 2) * tq])


def _sb_attend(qsb, sb_b, tq=SB_TQ):
    b, t, _ = qsb.shape
    tq = min(tq, t)
    nt = t // tq
    return pl.pallas_call(
        functools.partial(_sb_kernel, tq=tq),
        grid=(b, nt),
        in_specs=[pl.BlockSpec((1, tq, 256), lambda i, j: (i, j, 0)), pl.BlockSpec((1, t, 512), lambda i, j: (i, 0, 0))],
        out_specs=pl.BlockSpec((tq, 256), lambda i, j: (i * nt + j, 0)),
        out_shape=jax.ShapeDtypeStruct((b * t, 256), F32),
        scratch_shapes=[pltpu.VMEM((SB_HEADS * tq, LANES), F32)] * 2,
        compiler_params=_cparams(("parallel", "arbitrary")),
        name="sb_attend",
    )(qsb, sb_b)


def _memkv_kernel(x_ref, w_ref, o_ref, ot_ref):
    y = _dot(x_ref[0].astype(BF16), w_ref[...])
    o_ref[0] = y
    ot_ref[0] = y.T


def _memkv(mem, w):
    b, m, k = mem.shape
    n = w.shape[1]
    return pl.pallas_call(
        _memkv_kernel,
        grid=(b,),
        in_specs=[pl.BlockSpec((1, m, k), lambda i: (i, 0, 0)), pl.BlockSpec(w.shape, lambda i: (0, 0))],
        out_specs=[pl.BlockSpec((1, m, n), lambda i: (i, 0, 0)), pl.BlockSpec((1, n, m), lambda i: (i, 0, 0))],
        out_shape=[jax.ShapeDtypeStruct((b, m, n), F32), jax.ShapeDtypeStruct((b, n, m), F32)],
        compiler_params=_cparams(("parallel",)),
        name="memkv",
    )(mem, w)


def _mem_kernel(q_ref, mkv_ref, out_ref, *, tq):
    low = _lane((tq, LANES)) < HEAD_DIM
    for c in range(MEM_HEADS // 2):
        kk = mkv_ref[0, :, c * LANES:(c + 1) * LANES].astype(BF16)
        vv = mkv_ref[0, :, 256 + c * LANES:256 + (c + 1) * LANES].astype(BF16)
        ch = q_ref[0, :, c * LANES:(c + 1) * LANES]
        o = []
        for half in range(2):
            qm = jnp.where(low if half == 0 else ~low, ch, jnp.zeros_like(ch))
            s = _dot_nt(qm, kk)
            e = jnp.exp(s - jnp.max(s, axis=1, keepdims=True))
            p = e / jnp.sum(e, axis=1, keepdims=True)
            o.append(_dot(p.astype(BF16), vv))
        out_ref[:, c * LANES:(c + 1) * LANES] = jnp.where(low, o[0], o[1])


def _mem_attend(qmem, mkv, tq=MEM_TQ):
    b, t, _ = qmem.shape
    tq = min(tq, t)
    nt = t // tq
    return pl.pallas_call(
        functools.partial(_mem_kernel, tq=tq),
        grid=(b, nt),
        in_specs=[pl.BlockSpec((1, tq, 256), lambda i, j: (i, j, 0)),
                  pl.BlockSpec((1, N_MEM, 512), lambda i, j: (i, 0, 0))],
        out_specs=pl.BlockSpec((tq, 256), lambda i, j: (i * nt + j, 0)),
        out_shape=jax.ShapeDtypeStruct((b * t, 256), F32),
        compiler_params=_cparams(("parallel", "parallel")),
        name="mem_attend",
    )(qmem, mkv)


ROW_TILE = D_MODEL // LANES


def _store_tile_rows(ref, x):
    m = x.shape[0]
    for s in range(ROW_TILE):
        ref[pl.ds(s, m, stride=ROW_TILE), :] = x[:, s * LANES:(s + 1) * LANES]


def _load_tile_rows(ref, m, s):
    return ref[pl.ds(s, m, stride=ROW_TILE), :]


def _layer_norm(y, g, b):
    mu = jnp.mean(y, axis=1, keepdims=True)
    d = y - mu
    var = jnp.mean(d * d, axis=1, keepdims=True)
    return d * lax.rsqrt(var + LN_EPS) * g + b


def _outproj_kernel(*refs, n_tiles, tail_rows):
    if tail_rows:
        ins, (h_tail_ref, route_tail_ref, h_ref, route_ref) = refs[:9], refs[9:]
        i = pl.program_id(0)
        pl.when(i < n_tiles)(functools.partial(_outproj_tile, *ins, h_ref, route_ref))

        @pl.when(i == n_tiles)
        def _():
            for src, dst in ((h_tail_ref, h_ref), (route_tail_ref, route_ref)):
                dst[0:src.shape[0], :] = src[...]
                dst[src.shape[0]:, :] = jnp.zeros((dst.shape[0] - src.shape[0], dst.shape[1]), F32)
    else:
        _outproj_tile(*refs)


def _outproj_tile(onsa_ref, osb_ref, omem_ref, x_ref, w_ref, g_ref, b_ref, wr_ref, br_ref, h_ref, route_ref):
    mix = (_dot(onsa_ref[...].astype(BF16), w_ref[0:512, :]) + _dot(osb_ref[...].astype(BF16), w_ref[512:768, :])
           + _dot(omem_ref[...].astype(BF16), w_ref[768:1024, :]))
    h = _layer_norm(DEEPNORM_ALPHA * x_ref[...] + mix, g_ref[...], b_ref[...])
    _store_tile_rows(h_ref, h)
    logits = jnp.dot(h, wr_ref[...], preferred_element_type=F32, precision=lax.Precision.HIGHEST) + br_ref[...]
    tm = h.shape[0]
    lane = _lane((tm, LANES))
    lane_f = lane.astype(F32)
    route = jnp.zeros((tm, LANES), F32)
    es, denom, mx0 = [], 0.0, None
    for r in range(TOP_K):
        mx = jnp.max(logits, axis=1, keepdims=True)
        am = jnp.min(jnp.where(logits == mx, lane_f, float(LANES)), axis=1, keepdims=True)
        logits = jnp.where(lane_f == am, -jnp.inf, logits)
        mx0 = mx if r == 0 else mx0
        e = jnp.exp(mx - mx0)
        es.append(e)
        denom = denom + e
        route = jnp.where(lane == TOP_K + r, am, route)
    for r in range(TOP_K):
        route = jnp.where(lane == r, es[r] / denom, route)
    route_ref[...] = route


def _outproj(onsa, osb, omem, x, w_out_p, ln_g, ln_b, wr, br, tm=OUT_TM, tail=None):
    n = x.shape[0]
    tm = min(tm, n)
    n_tiles = n // tm
    tail = () if tail is None else tuple(tail)
    tail_rows = tail[1].shape[0] if tail else 0
    assert tail_rows <= tm
    row = lambda w: pl.BlockSpec((tm, w), lambda i: (jnp.minimum(i, n_tiles - 1), 0))
    const = lambda a: pl.BlockSpec(a.shape, lambda i: (0,) * a.ndim)
    n_out = n + (tm if tail else 0)
    return pl.pallas_call(
        functools.partial(_outproj_kernel, n_tiles=n_tiles, tail_rows=tail_rows),
        grid=(n_out // tm,),
        in_specs=[row(512), row(256), row(256), row(D_MODEL), const(w_out_p), const(ln_g), const(ln_b), const(wr),
                  const(br)] + [const(a) for a in tail],
        out_specs=[pl.BlockSpec((tm * ROW_TILE, LANES), lambda i: (i, 0)), pl.BlockSpec((tm, LANES), lambda i: (i, 0))],
        out_shape=[jax.ShapeDtypeStruct((n_out * ROW_TILE, LANES), F32), jax.ShapeDtypeStruct((n_out, LANES), F32)],
        compiler_params=_cparams(("parallel",)),
        name="outproj",
    )(onsa, osb, omem, x, w_out_p, ln_g, ln_b, wr, br, *tail)


def _expert_kernel(be_ref, rows_hbm, h_hbm, wgu_ref, bgu_ref, wd_ref, bd_ref, ys_hbm,
                   xbuf, ybuf, idx, wgu_b, wd_b, sem_i, sem_x, sem_y, *, bm, n_blocks, n_tok, plane):
    i = pl.program_id(0)
    slot = i % 2
    n_chunks = D_FF // 256
    per_chunk = bm // n_chunks

    ring = lambda blk: (blk + 3) % 3
    par = lambda blk: (blk + 2) % 2

    def idx_copy(blk):
        b = jnp.minimum(blk, n_blocks - 1)
        return pltpu.make_async_copy(rows_hbm.at[pl.ds(b, 1)], idx.at[pl.ds(ring(blk), 1)], sem_i.at[ring(blk)])

    def tile(ref, first_sublane):
        return ref.at[pl.ds(pl.multiple_of(first_sublane, ROW_TILE), ROW_TILE), :]

    def gather_row(blk, r):
        return pltpu.make_async_copy(tile(h_hbm, idx[ring(blk), 0, r]), tile(xbuf.at[par(blk)], r * ROW_TILE),
                                     sem_x.at[par(blk)])

    def scatter_row(blk, r):
        return pltpu.make_async_copy(tile(ybuf.at[par(blk)], r * ROW_TILE), tile(ys_hbm, idx[ring(blk), 1, r]),
                                     sem_y.at[par(blk)])

    def block_bytes_wait(buf, sem, s):
        pltpu.make_async_copy(h_hbm.at[pl.ds(0, bm * ROW_TILE), :], buf.at[s], sem.at[s]).wait()

    @pl.when(i == 0)
    def _():
        idx_copy(0).start()
        idx_copy(1).start()

        def spare(r, c):
            idx[2, 1, r] = ((r % TOP_K) * plane + n_tok + r // TOP_K) * ROW_TILE
            return c
        lax.fori_loop(0, bm, spare, 0)
        ybuf[1] = jnp.zeros((bm * ROW_TILE, LANES), F32)
        idx_copy(0).wait()

        def first(r, c):
            gather_row(0, r).start()
            return c
        lax.fori_loop(0, bm, first, 0)

    block_bytes_wait(xbuf, sem_x, slot)

    @pl.when((i == 0) | (be_ref[i] != be_ref[jnp.maximum(i - 1, 0)]))
    def _():
        wgu_b[...] = wgu_ref[0].astype(BF16)
        wd_b[...] = wd_ref[0].astype(BF16)

    idx_copy(i + 1).wait()

    @pl.when(i >= 1)
    def _():
        block_bytes_wait(ybuf, sem_y, slot)

    x = jnp.concatenate([_load_tile_rows(xbuf.at[slot], bm, s) for s in range(ROW_TILE)], axis=1).astype(BF16)
    y = bd_ref[0]
    for c in range(n_chunks):
        cols = slice(c * 256, (c + 1) * 256)
        ucols = slice(D_FF + c * 256, D_FF + (c + 1) * 256)
        g = jnp.minimum(_dot(x, wgu_b[:, cols]) + bgu_ref[0, :, cols], SWIGLU_LIMIT)
        u = jnp.clip(_dot(x, wgu_b[:, ucols]) + bgu_ref[0, :, ucols], -SWIGLU_LIMIT, SWIGLU_LIMIT)
        act = (u + 1.0) * g * jax.nn.sigmoid(SWIGLU_ALPHA * g)
        y = y + _dot(act.astype(BF16), wd_b[cols, :])
        for r in range(c * per_chunk, (c + 1) * per_chunk):
            gather_row(i + 1, r).start()
            scatter_row(i - 1, r).start()
    _store_tile_rows(ybuf.at[slot], y)
    idx_copy(i + 2).start()

    @pl.when(i == n_blocks - 1)
    def _():
        def last(r, c):
            scatter_row(i, r).start()
            return c
        block_bytes_wait(ybuf, sem_y, 1 - slot)
        lax.fori_loop(0, bm, last, 0)
        block_bytes_wait(xbuf, sem_x, 1 - slot)
        idx_copy(i + 2).wait()
        block_bytes_wait(ybuf, sem_y, slot)


def _experts(block_e, row_slot, h_all, w_gate_up, b_gate_up, w_down, b_down, n_tok, bm):
    n_blocks = row_slot.shape[0]
    plane = n_tok + bm // TOP_K
    tok = row_slot >> 2
    rows = jnp.stack([tok, (row_slot & (TOP_K - 1)) * plane + tok], axis=1) * ROW_TILE
    wmap = lambda i, be: (be[i], 0, 0)
    ys = pl.pallas_call(
        functools.partial(_expert_kernel, bm=bm, n_blocks=n_blocks, n_tok=n_tok, plane=plane),
        grid_spec=pltpu.PrefetchScalarGridSpec(
            num_scalar_prefetch=1,
            grid=(n_blocks,),
            in_specs=[pl.BlockSpec(memory_space=pl.ANY), pl.BlockSpec(memory_space=pl.ANY),
                      pl.BlockSpec((1, D_MODEL, 2 * D_FF), wmap), pl.BlockSpec((1, 1, 2 * D_FF), wmap),
                      pl.BlockSpec((1, D_FF, D_MODEL), wmap), pl.BlockSpec((1, 1, D_MODEL), wmap)],
            out_specs=pl.BlockSpec(memory_space=pl.ANY),
            scratch_shapes=[pltpu.VMEM((2, bm * ROW_TILE, LANES), F32), pltpu.VMEM((2, bm * ROW_TILE, LANES), F32),
                            pltpu.SMEM((3, 2, bm), I32),
                            pltpu.VMEM((D_MODEL, 2 * D_FF), BF16), pltpu.VMEM((D_FF, D_MODEL), BF16),
                            pltpu.SemaphoreType.DMA((3,)), pltpu.SemaphoreType.DMA((2,)),
                            pltpu.SemaphoreType.DMA((2,))],
        ),
        out_shape=jax.ShapeDtypeStruct((TOP_K * plane * ROW_TILE, LANES), F32),
        compiler_params=_cparams(("arbitrary",), manual_dma=True),
        name="experts",
    )(block_e, rows, h_all, w_gate_up, b_gate_up.reshape(N_EXPERTS, 1, 2 * D_FF), w_down,
      b_down.reshape(N_EXPERTS, 1, D_MODEL))
    return ys.reshape(TOP_K, plane * ROW_TILE, LANES)


def _combine_kernel(ys_ref, route_ref, h_ref, g_ref, b_ref, outp_ref, outs_ref, *, n_prompt_tiles):
    i = pl.program_id(0)
    route = route_ref[...]
    tm = route.shape[0]
    z = []
    for s in range(ROW_TILE):
        y = route[:, 0:1] * _load_tile_rows(ys_ref.at[0], tm, s)
        for r in range(1, TOP_K):
            y = y + route[:, r:r + 1] * _load_tile_rows(ys_ref.at[r], tm, s)
        z.append(DEEPNORM_ALPHA * _load_tile_rows(h_ref, tm, s) + y)
    out = _layer_norm(jnp.concatenate(z, axis=1), g_ref[...], b_ref[...])

    @pl.when(i < n_prompt_tiles)
    def _():
        outp_ref[...] = out

    @pl.when(i >= n_prompt_tiles)
    def _():
        outs_ref[...] = out


def _combine(ys, route_all, h_all, ln_g, ln_b, n_all, n_prompt, tm=COMBINE_TM):
    n_tiles = n_all // tm
    n_prompt_tiles = n_prompt // tm
    const = lambda a: pl.BlockSpec(a.shape, lambda i: (0,) * a.ndim)
    return pl.pallas_call(
        functools.partial(_combine_kernel, n_prompt_tiles=n_prompt_tiles),
        grid=(n_tiles,),
        in_specs=[pl.BlockSpec((TOP_K, tm * ROW_TILE, LANES), lambda i: (0, i, 0)),
                  pl.BlockSpec((tm, LANES), lambda i: (i, 0)),
                  pl.BlockSpec((tm * ROW_TILE, LANES), lambda i: (i, 0)), const(ln_g), const(ln_b)],
        out_specs=[pl.BlockSpec((tm, D_MODEL), lambda i: (jnp.minimum(i, n_prompt_tiles - 1), 0)),
                   pl.BlockSpec((tm, D_MODEL), lambda i: (jnp.maximum(i - n_prompt_tiles, 0), 0))],
        out_shape=[jax.ShapeDtypeStruct((n_prompt, D_MODEL), F32),
                   jax.ShapeDtypeStruct((n_all - n_prompt, D_MODEL), F32)],
        compiler_params=_cparams(("arbitrary",)),
        name="combine",
    )(ys, route_all, h_all, ln_g, ln_b)


def _moe(h_all, route_all, w_gate_up, b_gate_up, w_down, b_down, ln_g, ln_b, n_all, n_prompt, bm=MOE_BM):
    assert h_all.shape[0] >= (n_all + bm // TOP_K) * ROW_TILE and bm % TOP_K == 0
    n_slot = n_all * TOP_K
    flat_e = route_all[:n_all, TOP_K:2 * TOP_K].astype(I32).reshape(-1)
    order = jnp.argsort(flat_e, stable=True).astype(I32)
    counts = jnp.sum((flat_e[:, None] == jnp.arange(N_EXPERTS, dtype=I32)[None, :]).astype(I32), axis=0)
    padded = (counts + bm - 1) // bm * bm
    pad_end = jnp.cumsum(padded)
    pad_start = pad_end - padded
    start = jnp.cumsum(counts) - counts
    n_blocks = -(-(n_slot + N_EXPERTS * (bm - 1)) // bm)
    blk_row0 = jnp.arange(n_blocks, dtype=I32) * bm
    block_e = jnp.minimum(jnp.sum((pad_end[None, :] <= blk_row0[:, None]).astype(I32), axis=1), N_EXPERTS - 1)
    k = (blk_row0 - pad_start[block_e])[:, None] + jnp.arange(bm, dtype=I32)[None, :]
    valid = k < counts[block_e][:, None]
    src = jnp.clip(start[block_e][:, None] + k, 0, n_slot - 1)
    row_slot = jnp.where(valid, order[src], n_slot + jnp.arange(bm, dtype=I32)[None, :])
    ys = _experts(block_e, row_slot, h_all, w_gate_up, b_gate_up, w_down, b_down, n_all, bm)
    return _combine(ys, route_all, h_all, ln_g, ln_b, n_all, n_prompt)


def _cmp_sample_kernel(q8_ref, kvc_ref, ocmp_ref, idx_ref, *, sb, pos):
    blk = _lane((8, LANES))
    vis = blk * NSA_BLOCK + (NSA_BLOCK - 1) <= pos
    imps = []
    for s in range(sb):
        kc = kvc_ref[s, :, 0:128].astype(BF16)
        vc = kvc_ref[s, :, 128:256].astype(BF16)
        sc = jnp.where(vis, _dot_nt(q8_ref[s], kc), -jnp.inf)
        mx = jnp.max(sc, axis=1, keepdims=True)
        mx = jnp.where(mx > -jnp.inf, mx, 0.0)
        e = jnp.exp(sc - mx)
        p = e / jnp.maximum(jnp.sum(e, axis=1, keepdims=True), 1e-30)
        ocmp_ref[s] = _dot(p.astype(BF16), vc)
        for kvh in range(NSA_KV_HEADS):
            imps.append(jnp.sum(p[kvh * NSA_GROUP:(kvh + 1) * NSA_GROUP], axis=0, keepdims=True))
    imp = jnp.concatenate(imps, axis=0)
    blk2 = _lane((2 * sb, LANES))
    _, idx = _select_topk(imp, blk2 < (pos // NSA_BLOCK), blk2)
    idx_ref[0] = idx.astype(I32)


def _cmp_sample(q8c, kvc, pos, sb=SAMPLE_SB):
    n = q8c.shape[0]
    return pl.pallas_call(
        functools.partial(_cmp_sample_kernel, sb=sb, pos=pos),
        grid=(n // sb,),
        in_specs=[pl.BlockSpec((sb, 8, LANES), lambda i: (i, 0, 0)), pl.BlockSpec((sb, LANES, 256), lambda i: (i, 0, 0))],
        out_specs=[pl.BlockSpec((sb, 8, LANES), lambda i: (i, 0, 0)), pl.BlockSpec((1, 2 * sb, LANES), lambda i: (i, 0, 0))],
        out_shape=[jax.ShapeDtypeStruct((n, 8, LANES), F32), jax.ShapeDtypeStruct((n // sb, 2 * sb, LANES), I32)],
        compiler_params=_cparams(("parallel",)),
        name="cmp_sample",
    )(q8c, kvc)


def _attend_rows(s, v_t, s_self, v_self):
    mx = jnp.maximum(jnp.max(s, axis=1, keepdims=True), s_self)
    e = jnp.exp(s - mx)
    e_self = jnp.exp(s_self - mx)
    den = jnp.sum(e, axis=1, keepdims=True) + e_self
    num = _dot_nt(e.astype(BF16), v_t) + e_self.astype(BF16).astype(F32) * v_self
    return num / den


def _sample_kernel(pt_ref, sel_ref, q8r_ref, qsb_ref, qmem_ref, nsa_new_ref, win_new_ref, gate_ref, ocmp_ref,
                   cwin_ref, cmem_ref, nsa_hbm, sb_hbm, onsa_ref, osb_ref, omem_ref,
                   slc_buf, sb_buf, sem_slc, sem_sb, *, n_pages):
    i = pl.program_id(0)
    n_sel = NSA_KV_HEADS * NSA_TOPK

    def sel_block(j):
        return sel_ref[i * 2 * 16 + (j // NSA_TOPK) * 16 + j % NSA_TOPK]

    def slc_copy(j):
        page = pt_ref[i * n_pages + sel_block(j) // 2]
        return pltpu.make_async_copy(nsa_hbm.at[page, 256:512, :],
                                     slc_buf.at[j // NSA_TOPK, :, pl.ds((j % NSA_TOPK) * PAGE_SIZE, PAGE_SIZE)],
                                     sem_slc.at[0])

    def sb_copy(p):
        return pltpu.make_async_copy(sb_hbm.at[pt_ref[i * n_pages + p]], sb_buf.at[p % 2], sem_sb.at[p % 2])

    def issue(j, c):
        slc_copy(j).start()
        return c
    lax.fori_loop(0, n_sel, issue, 0)
    sb_copy(n_pages - 1).start()

    q8r = q8r_ref[0]
    row8 = _row((8, LANES))
    lane8 = _lane((8, LANES))
    head_half = (lane8 < HEAD_DIM) == (row8 < NSA_GROUP)

    kw = cwin_ref[0, 0:128, :].astype(BF16)
    vw = cwin_ref[0, 128:256, :].astype(BF16)
    k_new = win_new_ref[0, :, 0:128].astype(BF16).astype(F32)
    v_new = win_new_ref[0, :, 128:256].astype(BF16).astype(F32)
    s = jnp.where(_lane((8, NSA_WINDOW)) >= 1, _dot(q8r, kw), NEG)
    s_self = jnp.sum(q8r.astype(F32) * k_new, axis=1, keepdims=True)
    o_win = _attend_rows(s, vw, s_self, v_new)

    qm = qmem_ref[0]
    sm = _dot(qm, cmem_ref[0, 0:256, :].astype(BF16))
    em = jnp.exp(sm - jnp.max(sm, axis=1, keepdims=True))
    pm = em / jnp.sum(em, axis=1, keepdims=True)
    omem_ref[0] = _dot_nt(pm.astype(BF16), cmem_ref[0, 256:512, :].astype(BF16))

    def wait_slc(j, c):
        slc_copy(j).wait()
        return c
    lax.fori_loop(0, n_sel, wait_slc, 0)
    k_new = nsa_new_ref[0, :, 256:384].astype(BF16).astype(F32)
    v_new = nsa_new_ref[0, :, 384:512].astype(BF16).astype(F32)
    s_self = jnp.sum(q8r.astype(F32) * k_new, axis=1, keepdims=True)
    o_k = []
    for kvh in range(NSA_KV_HEADS):
        in_block = jnp.concatenate(
            [(lane8 // NSA_BLOCK) == (sel_block(kvh * NSA_TOPK + j) % 2) for j in range(NSA_TOPK)], axis=1)
        sc = jnp.where(in_block, _dot(q8r, slc_buf[kvh, 0:128, :].astype(BF16)), NEG)
        o_k.append(_attend_rows(sc, slc_buf[kvh, 128:256, :].astype(BF16), s_self, v_new))
    o_slc = jnp.where(row8 < NSA_GROUP, o_k[0], o_k[1])
    gate = gate_ref[0]
    merged = gate[:, 0:1] * ocmp_ref[0] + gate[:, 1:2] * o_slc + gate[:, 2:3] * o_win
    onsa_ref[0] = jnp.where(head_half, merged, 0.0)

    qs = qsb_ref[0]
    u2 = _suffix_matrix(PAGE_SIZE)
    all_keys = jnp.full((8, PAGE_SIZE), True)

    def cond(state):
        p, top, _, _ = state
        return (p >= 0) & (top > SB_EXIT)

    def body(state):
        p, _, carry, acc = state
        sb_copy(p).wait()

        @pl.when(p > 0)
        def _():
            sb_copy(p - 1).start()
        a, carry = _sb_tile(_dot(qs, sb_buf[p % 2, 0:256, :].astype(BF16)), all_keys, carry, u2)
        contrib = _dot_nt(a, sb_buf[p % 2, 256:512, :].astype(BF16))
        return p - 1, jnp.max(carry[0:SB_HEADS]), carry, acc + contrib

    p_end, _, _, acc = lax.while_loop(
        cond, body, (n_pages - 1, jnp.zeros((), F32), jnp.zeros((8, LANES), F32), jnp.zeros((8, 256), F32)))

    @pl.when(p_end >= 0)
    def _():
        sb_copy(p_end).wait()
    osb_ref[0] = acc


def _sample_attend(page_table, sel, q8r, qsb8, qmem8, nsa_new, win_new, gate8, ocmp8, cache_win, cache_mem,
                   cache_nsa, cache_sb):
    n = q8r.shape[0]
    n_pages = page_table.shape[0] // n
    per = lambda a: pl.BlockSpec((1,) + a.shape[1:], lambda i, pt, sl: (i,) + (0,) * (a.ndim - 1))
    vm = [q8r, qsb8, qmem8, nsa_new, win_new, gate8, ocmp8, cache_win, cache_mem]
    return pl.pallas_call(
        functools.partial(_sample_kernel, n_pages=n_pages),
        grid_spec=pltpu.PrefetchScalarGridSpec(
            num_scalar_prefetch=2,
            grid=(n,),
            in_specs=[per(a) for a in vm] + [pl.BlockSpec(memory_space=pl.ANY)] * 2,
            out_specs=[pl.BlockSpec((1, 8, LANES), lambda i, pt, sl: (i, 0, 0)),
                       pl.BlockSpec((1, 8, 256), lambda i, pt, sl: (i, 0, 0)),
                       pl.BlockSpec((1, 8, 256), lambda i, pt, sl: (i, 0, 0))],
            scratch_shapes=[pltpu.VMEM((NSA_KV_HEADS, 256, NSA_TOPK * PAGE_SIZE), F32),
                            pltpu.VMEM((2, 512, PAGE_SIZE), F32),
                            pltpu.SemaphoreType.DMA((1,)), pltpu.SemaphoreType.DMA((2,))],
        ),
        out_shape=[jax.ShapeDtypeStruct((n, 8, LANES), F32), jax.ShapeDtypeStruct((n, 8, 256), F32),
                   jax.ShapeDtypeStruct((n, 8, 256), F32)],
        compiler_params=_cparams(("arbitrary",)),
        name="sample_attend",
    )(page_table, sel, *vm, cache_nsa, cache_sb)


def _prep_mixer_weights(w_in, pe_cmp, w_cmp1, b_cmp1, w_cmp2, b_cmp2, w_out):
    head_perm = np.array([kvh * NSA_GROUP + g for g in range(NSA_GROUP) for kvh in range(NSA_KV_HEADS)])
    qcols = (head_perm[:, None] * HEAD_DIM + np.arange(HEAD_DIM)[None, :]).reshape(-1)
    w_q = w_in[:, 0:512][:, qcols]
    w_kv = w_in[:, 512:1280]
    w_g = jnp.pad(w_in[:, 1280:1304], ((0, 0), (0, LANES - 24)))
    w_all = jnp.concatenate([w_q, w_kv, w_in[:, 1304:2328], w_g], axis=1).astype(BF16)
    w_out_p = jnp.concatenate([w_out[0:512][qcols], w_out[512:]], axis=0).astype(BF16)
    w1 = jnp.transpose(w_cmp1.reshape(2, NSA_BLOCK, HEAD_DIM, CMP_HIDDEN), (0, 2, 1, 3))
    w1bd = jnp.zeros((2, HEAD_DIM, 2, NSA_BLOCK, 2, CMP_HIDDEN), F32)
    w2bd = jnp.zeros((2, 2, CMP_HIDDEN, 2, HEAD_DIM), F32)
    for blk in range(2):
        w1bd = w1bd.at[:, :, blk, :, blk, :].set(w1)
        w2bd = w2bd.at[:, blk, :, blk, :].set(w_cmp2)
    n_groups = HEAD_DIM // COMPRESS_DG
    w1bd = w1bd.reshape(2, n_groups, COMPRESS_DG * LANES, 2 * CMP_HIDDEN).astype(BF16)
    w2bd = w2bd.reshape(2, 2 * CMP_HIDDEN, LANES).astype(BF16)
    pe_t = jnp.tile(jnp.transpose(pe_cmp, (0, 2, 1)), (1, 1, 2)).reshape(2, n_groups, 1, COMPRESS_DG * LANES)
    b1t = jnp.tile(b_cmp1, (1, 2)).reshape(2, 1, 2 * CMP_HIDDEN)
    b2t = jnp.tile(b_cmp2, (1, 2)).reshape(2, 1, LANES)
    return w_all, w_out_p, (pe_t, w1bd, b1t, w2bd, b2t)


def _rope_tables(pos):
    half = HEAD_DIM // 2
    inv = ROPE_THETA ** (-jnp.arange(half, dtype=F32) / half)
    ang = pos.astype(F32)[:, None] * inv[None, :]
    cos, sin = jnp.cos(ang), jnp.sin(ang)
    cos_t = jnp.concatenate([cos, cos, cos, cos], axis=1)
    sin_t = jnp.concatenate([-sin, sin, -sin, sin], axis=1)
    return cos_t, sin_t


def _head_rows(q, n_heads, width):
    n = q.shape[0]
    lane_head = (np.arange(width) // HEAD_DIM)[None, None, :]
    rows = jnp.where(lane_head == np.arange(n_heads)[None, :, None], q[:, None, :], jnp.zeros((), q.dtype))
    return jnp.pad(rows, ((0, 0), (0, 8 - n_heads), (0, 0)))


def _nsa_head_rows(q):
    n = q.shape[0]
    q4 = q.reshape(n, NSA_GROUP, LANES)
    low = (np.arange(LANES) < HEAD_DIM)[None, None, :]
    zero = jnp.zeros((), q.dtype)
    return jnp.concatenate([jnp.where(low, q4, zero), jnp.where(low, zero, q4)], axis=1)


def _diag_heads(o, n_heads):
    lane_head = (np.arange(o.shape[2]) // HEAD_DIM)[None, None, :]
    keep = lane_head == np.arange(8)[None, :, None]
    return jnp.sum(jnp.where(keep, o, 0.0), axis=1)


def kernel(x_prompt, x_sample, mem_prompt, cache_nsa, cache_sb, cache_win, cache_mem, page_table, w_in, pe_cmp, w_cmp1,
           b_cmp1, w_cmp2, b_cmp2, w_mem_kv, w_out, ln1_g, ln1_b, w_router, b_router, w_gate_up, b_gate_up, w_down,
           b_down, ln2_g, ln2_b):
    assert w_in.shape[0] == DEPTH
    b, t, d = x_prompt.shape
    db = x_sample.shape[0]
    n = b * t
    n_pages = page_table.shape[1]
    n_phys = cache_nsa.shape[1]
    past = n_pages * PAGE_SIZE
    w_all, w_out_p, cw = _prep_mixer_weights(w_in[0], pe_cmp[0], w_cmp1[0], b_cmp1[0], w_cmp2[0], b_cmp2[0], w_out[0])
    ln1 = (ln1_g[0].reshape(1, d), ln1_b[0].reshape(1, d))
    ln2 = (ln2_g[0].reshape(1, d), ln2_b[0].reshape(1, d))
    wr = jnp.pad(w_router[0], ((0, 0), (0, LANES - N_EXPERTS)))
    br = jnp.concatenate([b_router[0], jnp.full((LANES - N_EXPERTS,), NEG, F32)]).reshape(1, LANES)

    cos_t, sin_t = _rope_tables(jnp.arange(t, dtype=I32))
    qc, qr, nsa_t, win_t, sb_t, slc_b, win_b, sb_b, qsb, qmem, gate = _project(x_prompt.reshape(n, d), w_all, cos_t,
                                                                               sin_t, b, t)
    kvc = _blocks_from_compress(_compress(nsa_t, jnp.arange(n // PAGE_SIZE, dtype=I32), *cw), b)
    ocmp, bias = _cmp_attend(qc.reshape(b, t, 512), kvc)
    onsa = _nsa_attend(qr.reshape(b, t, 512), bias, ocmp, gate, slc_b.reshape(b, t, 256), win_b.reshape(b, t, 256))
    osb = _sb_attend(qsb.reshape(b, t, 256), sb_b.reshape(b, t, 512))
    mkv, mkv_t = _memkv(mem_prompt, w_mem_kv[0].astype(BF16))
    omem = _mem_attend(qmem.reshape(b, t, 256), mkv)

    feature_major = lambda c: jnp.transpose(c[0], (0, 2, 3, 4, 1)).reshape(c.shape[1], -1, c.shape[2])
    cos_s, sin_s = _rope_tables(jnp.full((db,), past, I32))
    qc_s, qr_s, nsa_ts, win_ts, sb_ts, _, _, _, qsb_s, qmem_s, gate_s = _project(x_sample.reshape(db, d), w_all, cos_s,
                                                                               sin_s, 1, db)
    nsa_s, win_s = nsa_ts[0].T, win_ts[0].T
    pages_nsa = feature_major(cache_nsa)
    pages_sb = feature_major(cache_sb)
    pt_flat = page_table.reshape(-1)
    kvc_s = _blocks_from_compress(_compress(pages_nsa, pt_flat, *cw), db)
    ocmp8, idx = _cmp_sample(_nsa_head_rows(qc_s), kvc_s, past)
    sel = idx[:, :, :16].reshape(-1)
    gate8 = jnp.pad(gate_s[:, :24].reshape(db, 8, 3), ((0, 0), (0, 0), (0, LANES - 3)))
    onsa8, osb8, omem8 = _sample_attend(
        pt_flat, sel, _nsa_head_rows(qr_s), _head_rows(qsb_s, SB_HEADS, 256), _head_rows(qmem_s, MEM_HEADS, 256),
        nsa_s.reshape(db, 1, 512), win_s.reshape(db, 1, 256), gate8, ocmp8,
        feature_major(cache_win), feature_major(cache_mem), pages_nsa, pages_sb)
    onsa_s = (onsa8[:, :NSA_GROUP] + onsa8[:, NSA_GROUP:]).reshape(db, 512)
    tail = _outproj(onsa_s, _diag_heads(osb8, SB_HEADS), _diag_heads(omem8, MEM_HEADS), x_sample.reshape(db, d),
                    w_out_p, *ln1, wr, br)

    h_all, route_all = _outproj(onsa, osb, omem, x_prompt.reshape(n, d), w_out_p, *ln1, wr, br, tail=tail)
    yp, ys = _moe(h_all, route_all, w_gate_up[0], b_gate_up[0], w_down[0], b_down[0], *ln2, n + db, n)

    def rows(x_t, kinds, heads):
        s, _, p = x_t.shape
        return jnp.transpose(x_t.reshape(s, kinds, heads, HEAD_DIM, p), (0, 4, 1, 2, 3))[None]

    win_len = min(NSA_WINDOW, t)
    win_all_t = jnp.concatenate([feature_major(cache_win), win_s[:, :, None]], axis=2)[:, :, 1:]
    return (yp.reshape(b, t, d), ys.reshape(db, 1, d),
            rows(nsa_t, 4, NSA_KV_HEADS), rows(win_t[:, :, t - win_len:], 2, NSA_KV_HEADS),
            rows(sb_t, 2, SB_HEADS), rows(mkv_t, 2, MEM_HEADS),
            jnp.transpose(rows(nsa_ts, 4, NSA_KV_HEADS), (0, 2, 1, 3, 4, 5)), rows(win_all_t, 2, NSA_KV_HEADS),
            jnp.transpose(rows(sb_ts, 2, SB_HEADS), (0, 2, 1, 3, 4, 5)))
```

```python
import functools

import jax
import jax.numpy as jnp
import numpy as np
from jax import lax
from jax.experimental import pallas as pl
from jax.experimental.pallas import tpu as pltpu

F32 = jnp.float32
BF16 = jnp.bfloat16
I32 = jnp.int32

D_MODEL = 1024
HEAD_DIM = 64
NSA_HEADS = 8
NSA_KV_HEADS = 2
NSA_GROUP = 4
SB_HEADS = 4
MEM_HEADS = 4
N_MEM = 256
NSA_BLOCK = 64
NSA_TOPK = 15
NSA_WINDOW = 512
CMP_HIDDEN = 128
PAGE_SIZE = 128
ROPE_THETA = 10000.0
N_EXPERTS = 32
TOP_K = 4
D_FF = 1024
SWIGLU_LIMIT = 7.0
SWIGLU_ALPHA = 1.702
LN_EPS = 1e-5
DEPTH = 1
DEEPNORM_ALPHA = (2.0 * DEPTH) ** 0.25
SCALE = HEAD_DIM ** -0.5

LANES = 128
NEG = -1e30
M_INIT = -1e20
SB_EXIT = -100.0
VMEM_LIMIT = 56 * 1024 * 1024

PROJ_TM = 256
COMPRESS_PG = 64
COMPRESS_DG = 8
CMP_TQ = 256
NSA_TQ = 256
NSA_TK = 512
SB_TQ = 256
MEM_TQ = 256
OUT_TM = 256
MOE_BM = 256
COMBINE_TM = 128
SAMPLE_SB = 8

C_Q, C_KV, C_QSB, C_KSB, C_VSB, C_QMEM, C_G, C_END = 0, 512, 1280, 1536, 1792, 2048, 2304, 2432


def _cparams(sem, manual_dma=False):
    return pltpu.CompilerParams(dimension_semantics=sem, vmem_limit_bytes=VMEM_LIMIT,
                                disable_bounds_checks=manual_dma)


def _dot_nt(a, b):
    return lax.dot_general(a, b, (((1,), (1,)), ((), ())), preferred_element_type=F32)


def _dot(a, b):
    return jnp.dot(a, b, preferred_element_type=F32)


def _lane(shape):
    return lax.broadcasted_iota(I32, shape, 1)


def _row(shape):
    return lax.broadcasted_iota(I32, shape, 0)


def _proj_kernel(x_ref, w_ref, cos_ref, sin_ref, qc_ref, qr_ref, nsa_ref, win_ref, sb_ref,
                 slc_b_ref, win_b_ref, sb_b_ref, qsb_ref, qmem_ref, gate_ref):
    xb = x_ref[...].astype(BF16)
    cos = cos_ref[...]
    sin = sin_ref[...]
    tm = xb.shape[0]
    first_half = (_lane((tm, LANES)) % HEAD_DIM) < (HEAD_DIM // 2)

    def mm(lo, hi):
        return _dot(xb, w_ref[:, lo:hi])

    def rope(v):
        partner = jnp.where(first_half, pltpu.roll(v, LANES - HEAD_DIM // 2, 1), pltpu.roll(v, HEAD_DIM // 2, 1))
        return v * cos + partner * sin

    q = mm(C_Q, C_KV)
    for c in range(4):
        ch = q[:, c * LANES:(c + 1) * LANES]
        qc_ref[:, c * LANES:(c + 1) * LANES] = (ch * SCALE).astype(BF16)
        qr_ref[:, c * LANES:(c + 1) * LANES] = (rope(ch) * SCALE).astype(BF16)
    kv = mm(C_KV, C_QSB)
    nsa_ref[0, 0:256, :] = kv[:, 0:256].T
    k_slc = rope(kv[:, 256:384])
    v_slc = kv[:, 384:512]
    nsa_ref[0, 256:384, :] = k_slc.T
    nsa_ref[0, 384:512, :] = v_slc.T
    slc_b_ref[:, 0:128] = k_slc.astype(BF16)
    slc_b_ref[:, 128:256] = v_slc.astype(BF16)
    k_win = rope(kv[:, 512:640])
    v_win = kv[:, 640:768]
    win_ref[0, 0:128, :] = k_win.T
    win_ref[0, 128:256, :] = v_win.T
    win_b_ref[:, 0:128] = k_win.astype(BF16)
    win_b_ref[:, 128:256] = v_win.astype(BF16)
    qsb_ref[...] = (mm(C_QSB, C_KSB) * SCALE).astype(BF16)
    sb = mm(C_KSB, C_QMEM)
    sb_ref[0] = sb.T
    sb_b_ref[...] = sb.astype(BF16)
    qmem_ref[...] = (mm(C_QMEM, C_G) * SCALE).astype(BF16)
    gate_ref[...] = jax.nn.sigmoid(mm(C_G, C_END))


def _project(x, w_all, cos_t, sin_t, b, t, tm=PROJ_TM):
    n = x.shape[0]
    tm = min(tm, t)
    nt = t // tm
    row = lambda w: pl.BlockSpec((tm, w), lambda i: (i, 0))
    tab = pl.BlockSpec((tm, LANES), lambda i: (i % nt, 0))
    outs = [(512, BF16, False), (512, BF16, False), (512, F32, True), (256, F32, True), (512, F32, True),
            (256, BF16, False), (256, BF16, False), (512, BF16, False), (256, BF16, False), (256, BF16, False),
            (128, F32, False)]
    spec = lambda w, fm: pl.BlockSpec((1, w, tm), lambda i: (i // nt, 0, i % nt)) if fm else row(w)
    shape = lambda w, dt, fm: jax.ShapeDtypeStruct((b, w, t) if fm else (n, w), dt)
    return pl.pallas_call(
        _proj_kernel,
        grid=(n // tm,),
        in_specs=[row(D_MODEL), pl.BlockSpec((D_MODEL, C_END), lambda i: (0, 0)), tab, tab],
        out_specs=[spec(w, fm) for w, _, fm in outs],
        out_shape=[shape(*o) for o in outs],
        compiler_params=_cparams(("parallel",)),
        name="proj",
    )(x, w_all, cos_t, sin_t)


def _compress_kernel(pt_ref, rows_hbm, pe_ref, w1_ref, b1_ref, w2_ref, b2_ref, out_ref, buf, sem, *, pg, n_steps, ppb):
    i = pl.program_id(0)
    m = 2 * pg
    slabs = [(kind, head) for kind in range(2) for head in range(NSA_KV_HEADS)]

    def slab_copy(step, slot, j, kind, head):
        page = pt_ref[step * pg + j]
        src = rows_hbm.at[page // ppb, pl.ds(kind * LANES + head * HEAD_DIM, HEAD_DIM),
                          pl.ds((page % ppb) * PAGE_SIZE, PAGE_SIZE)]
        return pltpu.make_async_copy(src, buf.at[slot, kind, :, NSA_KV_HEADS * j + head, :], sem.at[slot])

    def issue(step, slot):
        def body(j, c):
            for kind, head in slabs:
                slab_copy(step, slot, j, kind, head).start()
            return c
        lax.fori_loop(0, pg, body, 0)

    @pl.when(i == 0)
    def _():
        issue(0, 0)

    @pl.when(i + 1 < n_steps)
    def _():
        issue(i + 1, (i + 1) % 2)

    slot = i % 2

    def wait_body(j, c):
        for kind, head in slabs:
            slab_copy(i, slot, j, kind, head).wait()
        return c
    lax.fori_loop(0, pg, wait_body, 0)

    n_groups = w1_ref.shape[1]
    dg = HEAD_DIM // n_groups
    for kind in range(2):
        acc = b1_ref[kind]
        for g in range(n_groups):
            xk = jnp.concatenate([buf[slot, kind, g * dg + dd] for dd in range(dg)], axis=1)
            acc = acc + _dot((xk + pe_ref[kind, g]).astype(BF16), w1_ref[kind, g])
        h = jax.nn.gelu(acc)
        out_ref[kind] = _dot(h.astype(BF16), w2_ref[kind]) + b2_ref[kind]


def _compress(rows3, page_table, pe_t, w1bd, b1t, w2bd, b2t, pg=COMPRESS_PG):
    n_pages = page_table.shape[0]
    n_steps = n_pages // pg
    m = 2 * pg
    const = lambda a: pl.BlockSpec(a.shape, lambda i, pt: (0,) * a.ndim)
    return pl.pallas_call(
        functools.partial(_compress_kernel, pg=pg, n_steps=n_steps, ppb=rows3.shape[2] // PAGE_SIZE),
        grid_spec=pltpu.PrefetchScalarGridSpec(
            num_scalar_prefetch=1,
            grid=(n_steps,),
            in_specs=[pl.BlockSpec(memory_space=pl.ANY), const(pe_t), const(w1bd), const(b1t), const(w2bd), const(b2t)],
            out_specs=pl.BlockSpec((2, m, LANES), lambda i, pt: (0, i, 0)),
            scratch_shapes=[pltpu.VMEM((2, 2, HEAD_DIM, m, LANES), F32), pltpu.SemaphoreType.DMA((2,))],
        ),
        out_shape=jax.ShapeDtypeStruct((2, 2 * n_pages, LANES), F32),
        compiler_params=_cparams(("arbitrary",)),
        name="compress",
    )(page_table, rows3, pe_t, w1bd, b1t, w2bd, b2t)


def _blocks_from_compress(out, n_seq):
    n_pages = out.shape[1] // 2
    o = out.reshape(2, n_pages, NSA_KV_HEADS, 2, HEAD_DIM)
    o = jnp.transpose(o, (1, 3, 0, 2, 4))
    return o.reshape(n_seq, 2 * n_pages // n_seq, 256)


def _select_topk(imp, cand, blk):
    tq = imp.shape[0]
    blk_f = blk.astype(F32)
    lane = _lane((tq, LANES))
    impm = jnp.where(cand, imp, -1.0)
    sel = jnp.zeros((tq, LANES), F32)
    idx = jnp.zeros((tq, LANES), F32)
    for r in range(NSA_TOPK):
        mx = jnp.max(impm, axis=1, keepdims=True)
        am = jnp.min(jnp.where(impm == mx, blk_f, float(LANES)), axis=1, keepdims=True)
        hit = blk_f == am
        sel = jnp.where(hit & (mx >= 0.0), 1.0, sel)
        idx = jnp.where(lane == r, am, idx)
        impm = jnp.where(hit, -2.0, impm)
    return sel, idx


def _cmp_kernel(qc_ref, kvc_ref, ocmp_ref, bias_ref, *, tq):
    t0 = pl.program_id(1) * tq
    pos = t0 + _row((tq, LANES))
    blk = _lane((tq, LANES))
    low = blk < HEAD_DIM
    vis = blk * NSA_BLOCK + (NSA_BLOCK - 1) <= pos
    cur = pos >> 6
    cand = blk < cur
    kc = kvc_ref[0, :, 0:128].astype(BF16)
    vc = kvc_ref[0, :, 128:256].astype(BF16)
    imp = [jnp.zeros((tq, LANES), F32), jnp.zeros((tq, LANES), F32)]
    for g in range(NSA_GROUP):
        qch = qc_ref[0, :, g * LANES:(g + 1) * LANES]
        o = []
        for kvh in range(NSA_KV_HEADS):
            qm = jnp.where(low if kvh == 0 else ~low, qch, jnp.zeros_like(qch))
            s = jnp.where(vis, _dot_nt(qm, kc), -jnp.inf)
            mx = jnp.max(s, axis=1, keepdims=True)
            mx = jnp.where(mx > -jnp.inf, mx, 0.0)
            e = jnp.exp(s - mx)
            p = e / jnp.maximum(jnp.sum(e, axis=1, keepdims=True), 1e-30)
            imp[kvh] = imp[kvh] + p
            o.append(_dot(p.astype(BF16), vc))
        ocmp_ref[0, :, g * LANES:(g + 1) * LANES] = jnp.where(low, o[0], o[1])
    for kvh in range(NSA_KV_HEADS):
        sel, _ = _select_topk(imp[kvh], cand, blk)
        keep = (sel > 0.5) | (blk == cur)
        bias_ref[0, :, kvh * LANES:(kvh + 1) * LANES] = jnp.where(keep, 0.0, NEG).astype(BF16)


def _cmp_attend(qc, kvc, tq=CMP_TQ):
    b, t, _ = qc.shape
    tq = min(tq, t)
    return pl.pallas_call(
        functools.partial(_cmp_kernel, tq=tq),
        grid=(b, t // tq),
        in_specs=[pl.BlockSpec((1, tq, 512), lambda i, j: (i, j, 0)),
                  pl.BlockSpec((1, LANES, 256), lambda i, j: (i, 0, 0))],
        out_specs=[pl.BlockSpec((1, tq, 512), lambda i, j: (i, j, 0)),
                   pl.BlockSpec((1, tq, 256), lambda i, j: (i, j, 0))],
        out_shape=[jax.ShapeDtypeStruct((b, t, 512), F32), jax.ShapeDtypeStruct((b, t, 256), BF16)],
        compiler_params=_cparams(("parallel", "parallel")),
        name="cmp_attend",
    )(qc, kvc)


def _rep(x, k):
    return x if k == 1 else jnp.concatenate([x] * k, axis=1)


def _softmax_step(s, v_aug, m_ref, acc_ref):
    k = s.shape[1] // LANES
    m_prev = m_ref[...]
    m_new = jnp.maximum(m_prev, jnp.max(s, axis=1, keepdims=True))
    alpha = jnp.exp(m_prev - m_new)
    p = jnp.exp(s - _rep(m_new, k)).astype(BF16)
    acc_ref[...] = _rep(alpha, 2) * acc_ref[...] + _dot(p, v_aug)
    m_ref[...] = m_new


def _nsa_kernel(qr_ref, bias_ref, ocmp_ref, gate_ref, slc_ref, win_ref, out_ref, m_ref, acc_ref, *, tq, tk, band):
    j = pl.program_id(1)
    t = slc_ref.shape[1]
    t0 = j * tq
    rows = NSA_GROUP * tq
    low_q = _lane((tq, LANES)) < HEAD_DIM
    qpos = t0 + _row((tq, LANES))
    qpos_rows = jnp.concatenate([qpos] * NSA_GROUP, axis=0)
    heads = range(NSA_KV_HEADS)

    qm, qaug = [], []
    for kvh in heads:
        keep = low_q if kvh == 0 else ~low_q
        qm.append(jnp.concatenate(
            [jnp.where(keep, qr_ref[0, :, g * LANES:(g + 1) * LANES], jnp.zeros((tq, LANES), BF16))
             for g in range(NSA_GROUP)], axis=0))
        bias = bias_ref[0, :, kvh * LANES:(kvh + 1) * LANES]
        qaug.append(jnp.concatenate([qm[kvh], jnp.concatenate([bias] * NSA_GROUP, axis=0)], axis=1))

    m_ref[...] = jnp.full(m_ref.shape, M_INIT, F32)
    acc_ref[...] = jnp.zeros(acc_ref.shape, F32)
    ones_k = jnp.ones((tk, LANES), BF16)

    def slc_tile(kt, causal):
        k0 = pl.multiple_of(kt * tk, tk)
        kblk = (k0 + _row((tk, LANES))) >> 6
        onehot = jnp.where(kblk == _lane((tk, LANES)), 1.0, 0.0).astype(BF16)
        k_aug = jnp.concatenate([slc_ref[0, pl.ds(k0, tk), 0:128], onehot], axis=1)
        v_aug = jnp.concatenate([slc_ref[0, pl.ds(k0, tk), 128:256], ones_k], axis=1)
        for kvh in heads:
            s = _dot_nt(qaug[kvh], k_aug)
            if causal:
                s = jnp.where(k0 + _lane((rows, tk)) <= _rep(qpos_rows, tk // LANES), s, NEG)
            _softmax_step(s, v_aug, m_ref.at[kvh], acc_ref.at[kvh])

    n_open = t0 // tk
    n_tiles = (t0 + tq + tk - 1) // tk

    def open_body(kt, c):
        slc_tile(kt, False)
        return c

    def diag_body(kt, c):
        slc_tile(kt, True)
        return c
    lax.fori_loop(0, n_open, open_body, 0)
    lax.fori_loop(n_open, n_tiles, diag_body, 0)
    o_slc = [acc_ref[kvh, :, 0:LANES] / acc_ref[kvh, :, LANES:2 * LANES] for kvh in heads]

    w0 = pl.multiple_of(jnp.clip(t0 - NSA_WINDOW, 0, t - band), LANES)
    kw = win_ref[0, pl.ds(w0, band), 0:128]
    vw_aug = jnp.concatenate([win_ref[0, pl.ds(w0, band), 128:256], jnp.ones((band, LANES), BF16)], axis=1)
    kpos = w0 + _lane((rows, band))
    qp = _rep(qpos_rows, band // LANES)
    in_window = (kpos <= qp) & (kpos > qp - NSA_WINDOW)
    o_win = []
    for kvh in heads:
        s = jnp.where(in_window, _dot_nt(qm[kvh], kw), NEG)
        p = jnp.exp(s - jnp.max(s, axis=1, keepdims=True)).astype(BF16)
        o = _dot(p, vw_aug)
        o_win.append(o[:, 0:LANES] / o[:, LANES:2 * LANES])

    gate = gate_ref[...]

    def gate_col(g, c):
        a = gate[:, (0 * NSA_GROUP + g) * 3 + c:(0 * NSA_GROUP + g) * 3 + c + 1]
        b = gate[:, (1 * NSA_GROUP + g) * 3 + c:(1 * NSA_GROUP + g) * 3 + c + 1]
        return jnp.where(low_q, a, b)

    for g in range(NSA_GROUP):
        sl = slice(g * tq, (g + 1) * tq)
        slc = jnp.where(low_q, o_slc[0][sl], o_slc[1][sl])
        win = jnp.where(low_q, o_win[0][sl], o_win[1][sl])
        cmp_ = ocmp_ref[0, :, g * LANES:(g + 1) * LANES]
        out_ref[:, g * LANES:(g + 1) * LANES] = gate_col(g, 0) * cmp_ + gate_col(g, 1) * slc + gate_col(g, 2) * win


def _nsa_attend(qr, bias, ocmp, gate, slc_b, win_b, tq=NSA_TQ, tk=NSA_TK):
    b, t, _ = qr.shape
    tq, tk = min(tq, t), min(tk, t)
    nt = t // tq
    rows = NSA_GROUP * tq
    band = min(NSA_WINDOW + tq, t)
    tile = lambda w: pl.BlockSpec((1, tq, w), lambda i, j: (i, j, 0))
    full = lambda w: pl.BlockSpec((1, t, w), lambda i, j: (i, 0, 0))
    flat = lambda w: pl.BlockSpec((tq, w), lambda i, j: (i * nt + j, 0))
    return pl.pallas_call(
        functools.partial(_nsa_kernel, tq=tq, tk=tk, band=band),
        grid=(b, nt),
        in_specs=[tile(512), tile(256), tile(512), flat(LANES), full(256), full(256)],
        out_specs=flat(512),
        out_shape=jax.ShapeDtypeStruct((b * t, 512), F32),
        scratch_shapes=[pltpu.VMEM((NSA_KV_HEADS, rows, LANES), F32), pltpu.VMEM((NSA_KV_HEADS, rows, 2 * LANES), F32)],
        compiler_params=_cparams(("parallel", "arbitrary")),
        name="nsa_attend",
    )(qr, bias, ocmp, gate, slc_b, win_b)


def _log_sigmoid(z):
    return jnp.minimum(z, 0.0) - jnp.log1p(jnp.exp(-jnp.abs(z)))


def _suffix_matrix(tk):
    r = _row((tk, 2 * LANES))
    c = _lane((tk, 2 * LANES))
    return jnp.where((r > c) | (c >= LANES), 1.0, 0.0).astype(BF16)


def _sb_tile(z, mask, carry, u2):
    ls = _log_sigmoid(z)
    lk = jnp.where(mask, ls - z, 0.0)
    hi = lk.astype(BF16)
    lo = (lk - hi.astype(F32)).astype(BF16)
    cs = _dot(hi, u2) + _dot(lo, u2)
    a = jnp.where(mask, jnp.exp(ls + carry + cs[:, :LANES]), 0.0)
    return a.astype(BF16), carry + cs[:, LANES:]


def _sb_kernel(q_ref, kv_ref, out_ref, carry_ref, acc_ref, *, tq):
    j = pl.program_id(1)
    t0 = j * tq
    rows = SB_HEADS * tq
    low = _lane((tq, LANES)) < HEAD_DIM
    qpos_rows = jnp.concatenate([t0 + _row((tq, LANES))] * SB_HEADS, axis=0)
    u2 = _suffix_matrix(LANES)
    qpair = []
    for c in range(SB_HEADS // 2):
        ch = q_ref[0, :, c * LANES:(c + 1) * LANES]
        zero = jnp.zeros_like(ch)
        qpair.append(jnp.concatenate([jnp.where(low, ch, zero), jnp.where(low, zero, ch)], axis=0))
    carry_ref[...] = jnp.zeros_like(carry_ref)
    acc_ref[...] = jnp.zeros_like(acc_ref)

    def cond(state):
        kt, top = state
        return (kt >= 0) & (top > SB_EXIT)

    def body(state):
        kt, _ = state
        k0 = pl.multiple_of(kt * LANES, LANES)
        mask = (k0 + _lane((rows, LANES))) < qpos_rows
        z = jnp.concatenate([_dot_nt(qpair[c], kv_ref[0, pl.ds(k0, LANES), c * LANES:(c + 1) * LANES])
                             for c in range(SB_HEADS // 2)], axis=0)
        a, carry = _sb_tile(z, mask, carry_ref[...], u2)
        contrib = jnp.concatenate(
            [_dot(a[2 * c * tq:2 * (c + 1) * tq], kv_ref[0, pl.ds(k0, LANES), 256 + c * LANES:256 + (c + 1) * LANES])
             for c in range(SB_HEADS // 2)], axis=0)
        acc_ref[...] = acc_ref[...] + contrib
        carry_ref[...] = carry
        return kt - 1, jnp.max(carry)

    lax.while_loop(cond, body, ((t0 + tq) // LANES - 1, jnp.zeros((), F32)))
    for c in range(SB_HEADS // 2):
        out_ref[:, c * LANES:(c + 1) * LANES] = jnp.where(low, acc_ref[2 * c * tq:(2 * c + 1) * tq],
                                                          acc_ref[(2 * c + 1) * tq:(2 * c + 2) * tq])


def _sb_attend(qsb, sb_b, tq=SB_TQ):
    b, t, _ = qsb.shape
    tq = min(tq, t)
    nt = t // tq
    return pl.pallas_call(
        functools.partial(_sb_kernel, tq=tq),
        grid=(b, nt),
        in_specs=[pl.BlockSpec((1, tq, 256), lambda i, j: (i, j, 0)), pl.BlockSpec((1, t, 512), lambda i, j: (i, 0, 0))],
        out_specs=pl.BlockSpec((tq, 256), lambda i, j: (i * nt + j, 0)),
        out_shape=jax.ShapeDtypeStruct((b * t, 256), F32),
        scratch_shapes=[pltpu.VMEM((SB_HEADS * tq, LANES), F32)] * 2,
        compiler_params=_cparams(("parallel", "arbitrary")),
        name="sb_attend",
    )(qsb, sb_b)


def _memkv_kernel(x_ref, w_ref, o_ref, ot_ref):
    y = _dot(x_ref[0].astype(BF16), w_ref[...])
    o_ref[0] = y
    ot_ref[0] = y.T


def _memkv(mem, w):
    b, m, k = mem.shape
    n = w.shape[1]
    return pl.pallas_call(
        _memkv_kernel,
        grid=(b,),
        in_specs=[pl.BlockSpec((1, m, k), lambda i: (i, 0, 0)), pl.BlockSpec(w.shape, lambda i: (0, 0))],
        out_specs=[pl.BlockSpec((1, m, n), lambda i: (i, 0, 0)), pl.BlockSpec((1, n, m), lambda i: (i, 0, 0))],
        out_shape=[jax.ShapeDtypeStruct((b, m, n), F32), jax.ShapeDtypeStruct((b, n, m), F32)],
        compiler_params=_cparams(("parallel",)),
        name="memkv",
    )(mem, w)


def _mem_kernel(q_ref, mkv_ref, out_ref, *, tq):
    low = _lane((tq, LANES)) < HEAD_DIM
    for c in range(MEM_HEADS // 2):
        kk = mkv_ref[0, :, c * LANES:(c + 1) * LANES].astype(BF16)
        vv = mkv_ref[0, :, 256 + c * LANES:256 + (c + 1) * LANES].astype(BF16)
        ch = q_ref[0, :, c * LANES:(c + 1) * LANES]
        o = []
        for half in range(2):
            qm = jnp.where(low if half == 0 else ~low, ch, jnp.zeros_like(ch))
            s = _dot_nt(qm, kk)
            e = jnp.exp(s - jnp.max(s, axis=1, keepdims=True))
            p = e / jnp.sum(e, axis=1, keepdims=True)
            o.append(_dot(p.astype(BF16), vv))
        out_ref[:, c * LANES:(c + 1) * LANES] = jnp.where(low, o[0], o[1])


def _mem_attend(qmem, mkv, tq=MEM_TQ):
    b, t, _ = qmem.shape
    tq = min(tq, t)
    nt = t // tq
    return pl.pallas_call(
        functools.partial(_mem_kernel, tq=tq),
        grid=(b, nt),
        in_specs=[pl.BlockSpec((1, tq, 256), lambda i, j: (i, j, 0)),
                  pl.BlockSpec((1, N_MEM, 512), lambda i, j: (i, 0, 0))],
        out_specs=pl.BlockSpec((tq, 256), lambda i, j: (i * nt + j, 0)),
        out_shape=jax.ShapeDtypeStruct((b * t, 256), F32),
        compiler_params=_cparams(("parallel", "parallel")),
        name="mem_attend",
    )(qmem, mkv)


ROW_TILE = D_MODEL // LANES


def _store_tile_rows(ref, x):
    m = x.shape[0]
    for s in range(ROW_TILE):
        ref[pl.ds(s, m, stride=ROW_TILE), :] = x[:, s * LANES:(s + 1) * LANES]


def _load_tile_rows(ref, m, s):
    return ref[pl.ds(s, m, stride=ROW_TILE), :]


def _layer_norm(y, g, b):
    mu = jnp.mean(y, axis=1, keepdims=True)
    d = y - mu
    var = jnp.mean(d * d, axis=1, keepdims=True)
    return d * lax.rsqrt(var + LN_EPS) * g + b


def _outproj_kernel(*refs, n_tiles, tail_rows):
    if tail_rows:
        ins, (h_tail_ref, route_tail_ref, h_ref, route_ref) = refs[:9], refs[9:]
        i = pl.program_id(0)
        pl.when(i < n_tiles)(functools.partial(_outproj_tile, *ins, h_ref, route_ref))

        @pl.when(i == n_tiles)
        def _():
            for src, dst in ((h_tail_ref, h_ref), (route_tail_ref, route_ref)):
                dst[0:src.shape[0], :] = src[...]
                dst[src.shape[0]:, :] = jnp.zeros((dst.shape[0] - src.shape[0], dst.shape[1]), F32)
    else:
        _outproj_tile(*refs)


def _outproj_tile(onsa_ref, osb_ref, omem_ref, x_ref, w_ref, g_ref, b_ref, wr_ref, br_ref, h_ref, route_ref):
    mix = (_dot(onsa_ref[...].astype(BF16), w_ref[0:512, :]) + _dot(osb_ref[...].astype(BF16), w_ref[512:768, :])
           + _dot(omem_ref[...].astype(BF16), w_ref[768:1024, :]))
    h = _layer_norm(DEEPNORM_ALPHA * x_ref[...] + mix, g_ref[...], b_ref[...])
    _store_tile_rows(h_ref, h)
    logits = jnp.dot(h, wr_ref[...], preferred_element_type=F32, precision=lax.Precision.HIGHEST) + br_ref[...]
    tm = h.shape[0]
    lane = _lane((tm, LANES))
    lane_f = lane.astype(F32)
    route = jnp.zeros((tm, LANES), F32)
    es, denom, mx0 = [], 0.0, None
    for r in range(TOP_K):
        mx = jnp.max(logits, axis=1, keepdims=True)
        am = jnp.min(jnp.where(logits == mx, lane_f, float(LANES)), axis=1, keepdims=True)
        logits = jnp.where(lane_f == am, -jnp.inf, logits)
        mx0 = mx if r == 0 else mx0
        e = jnp.exp(mx - mx0)
        es.append(e)
        denom = denom + e
        route = jnp.where(lane == TOP_K + r, am, route)
    for r in range(TOP_K):
        route = jnp.where(lane == r, es[r] / denom, route)
    route_ref[...] = route


def _outproj(onsa, osb, omem, x, w_out_p, ln_g, ln_b, wr, br, tm=OUT_TM, tail=None):
    n = x.shape[0]
    tm = min(tm, n)
    n_tiles = n // tm
    tail = () if tail is None else tuple(tail)
    tail_rows = tail[1].shape[0] if tail else 0
    assert tail_rows <= tm
    row = lambda w: pl.BlockSpec((tm, w), lambda i: (jnp.minimum(i, n_tiles - 1), 0))
    const = lambda a: pl.BlockSpec(a.shape, lambda i: (0,) * a.ndim)
    n_out = n + (tm if tail else 0)
    return pl.pallas_call(
        functools.partial(_outproj_kernel, n_tiles=n_tiles, tail_rows=tail_rows),
        grid=(n_out // tm,),
        in_specs=[row(512), row(256), row(256), row(D_MODEL), const(w_out_p), const(ln_g), const(ln_b), const(wr),
                  const(br)] + [const(a) for a in tail],
        out_specs=[pl.BlockSpec((tm * ROW_TILE, LANES), lambda i: (i, 0)), pl.BlockSpec((tm, LANES), lambda i: (i, 0))],
        out_shape=[jax.ShapeDtypeStruct((n_out * ROW_TILE, LANES), F32), jax.ShapeDtypeStruct((n_out, LANES), F32)],
        compiler_params=_cparams(("parallel",)),
        name="outproj",
    )(onsa, osb, omem, x, w_out_p, ln_g, ln_b, wr, br, *tail)


def _expert_kernel(be_ref, rows_hbm, h_hbm, wgu_ref, bgu_ref, wd_ref, bd_ref, ys_hbm,
                   xbuf, ybuf, idx, wgu_b, wd_b, sem_i, sem_x, sem_y, *, bm, n_blocks, n_tok, plane):
    i = pl.program_id(0)
    slot = i % 2
    n_chunks = D_FF // 256
    per_chunk = bm // n_chunks

    ring = lambda blk: (blk + 3) % 3
    par = lambda blk: (blk + 2) % 2

    def idx_copy(blk):
        b = jnp.minimum(blk, n_blocks - 1)
        return pltpu.make_async_copy(rows_hbm.at[pl.ds(b, 1)], idx.at[pl.ds(ring(blk), 1)], sem_i.at[ring(blk)])

    def tile(ref, first_sublane):
        return ref.at[pl.ds(pl.multiple_of(first_sublane, ROW_TILE), ROW_TILE), :]

    def gather_row(blk, r):
        return pltpu.make_async_copy(tile(h_hbm, idx[ring(blk), 0, r]), tile(xbuf.at[par(blk)], r * ROW_TILE),
                                     sem_x.at[par(blk)])

    def scatter_row(blk, r):
        return pltpu.make_async_copy(tile(ybuf.at[par(blk)], r * ROW_TILE), tile(ys_hbm, idx[ring(blk), 1, r]),
                                     sem_y.at[par(blk)])

    def block_bytes_wait(buf, sem, s):
        pltpu.make_async_copy(h_hbm.at[pl.ds(0, bm * ROW_TILE), :], buf.at[s], sem.at[s]).wait()

    @pl.when(i == 0)
    def _():
        idx_copy(0).start()
        idx_copy(1).start()

        def spare(r, c):
            idx[2, 1, r] = ((r % TOP_K) * plane + n_tok + r // TOP_K) * ROW_TILE
            return c
        lax.fori_loop(0, bm, spare, 0)
        ybuf[1] = jnp.zeros((bm * ROW_TILE, LANES), F32)
        idx_copy(0).wait()

        def first(r, c):
            gather_row(0, r).start()
            return c
        lax.fori_loop(0, bm, first, 0)

    block_bytes_wait(xbuf, sem_x, slot)

    @pl.when((i == 0) | (be_ref[i] != be_ref[jnp.maximum(i - 1, 0)]))
    def _():
        wgu_b[...] = wgu_ref[0].astype(BF16)
        wd_b[...] = wd_ref[0].astype(BF16)

    idx_copy(i + 1).wait()

    @pl.when(i >= 1)
    def _():
        block_bytes_wait(ybuf, sem_y, slot)

    x = jnp.concatenate([_load_tile_rows(xbuf.at[slot], bm, s) for s in range(ROW_TILE)], axis=1).astype(BF16)
    y = bd_ref[0]
    for c in range(n_chunks):
        cols = slice(c * 256, (c + 1) * 256)
        ucols = slice(D_FF + c * 256, D_FF + (c + 1) * 256)
        g = jnp.minimum(_dot(x, wgu_b[:, cols]) + bgu_ref[0, :, cols], SWIGLU_LIMIT)
        u = jnp.clip(_dot(x, wgu_b[:, ucols]) + bgu_ref[0, :, ucols], -SWIGLU_LIMIT, SWIGLU_LIMIT)
        act = (u + 1.0) * g * jax.nn.sigmoid(SWIGLU_ALPHA * g)
        y = y + _dot(act.astype(BF16), wd_b[cols, :])
        for r in range(c * per_chunk, (c + 1) * per_chunk):
            gather_row(i + 1, r).start()
            scatter_row(i - 1, r).start(priority=1)
    _store_tile_rows(ybuf.at[slot], y)
    idx_copy(i + 2).start()

    @pl.when(i == n_blocks - 1)
    def _():
        def last(r, c):
            scatter_row(i, r).start()
            return c
        block_bytes_wait(ybuf, sem_y, 1 - slot)
        lax.fori_loop(0, bm, last, 0)
        block_bytes_wait(xbuf, sem_x, 1 - slot)
        idx_copy(i + 2).wait()
        block_bytes_wait(ybuf, sem_y, slot)


def _experts(block_e, row_slot, h_all, w_gate_up, b_gate_up, w_down, b_down, n_tok, bm):
    n_blocks = row_slot.shape[0]
    plane = n_tok + bm // TOP_K
    tok = row_slot >> 2
    rows = jnp.stack([tok, (row_slot & (TOP_K - 1)) * plane + tok], axis=1) * ROW_TILE
    wmap = lambda i, be: (be[i], 0, 0)
    ys = pl.pallas_call(
        functools.partial(_expert_kernel, bm=bm, n_blocks=n_blocks, n_tok=n_tok, plane=plane),
        grid_spec=pltpu.PrefetchScalarGridSpec(
            num_scalar_prefetch=1,
            grid=(n_blocks,),
            in_specs=[pl.BlockSpec(memory_space=pl.ANY), pl.BlockSpec(memory_space=pl.ANY),
                      pl.BlockSpec((1, D_MODEL, 2 * D_FF), wmap), pl.BlockSpec((1, 1, 2 * D_FF), wmap),
                      pl.BlockSpec((1, D_FF, D_MODEL), wmap), pl.BlockSpec((1, 1, D_MODEL), wmap)],
            out_specs=pl.BlockSpec(memory_space=pl.ANY),
            scratch_shapes=[pltpu.VMEM((2, bm * ROW_TILE, LANES), F32), pltpu.VMEM((2, bm * ROW_TILE, LANES), F32),
                            pltpu.SMEM((3, 2, bm), I32),
                            pltpu.VMEM((D_MODEL, 2 * D_FF), BF16), pltpu.VMEM((D_FF, D_MODEL), BF16),
                            pltpu.SemaphoreType.DMA((3,)), pltpu.SemaphoreType.DMA((2,)),
                            pltpu.SemaphoreType.DMA((2,))],
        ),
        out_shape=jax.ShapeDtypeStruct((TOP_K * plane * ROW_TILE, LANES), F32),
        compiler_params=_cparams(("arbitrary",), manual_dma=True),
        name="experts",
    )(block_e, rows, h_all, w_gate_up, b_gate_up.reshape(N_EXPERTS, 1, 2 * D_FF), w_down,
      b_down.reshape(N_EXPERTS, 1, D_MODEL))
    return ys.reshape(TOP_K, plane * ROW_TILE, LANES)


def _combine_kernel(ys_ref, route_ref, h_ref, g_ref, b_ref, outp_ref, outs_ref, *, n_prompt_tiles):
    i = pl.program_id(0)
    route = route_ref[...]
    tm = route.shape[0]
    z = []
    for s in range(ROW_TILE):
        y = route[:, 0:1] * _load_tile_rows(ys_ref.at[0], tm, s)
        for r in range(1, TOP_K):
            y = y + route[:, r:r + 1] * _load_tile_rows(ys_ref.at[r], tm, s)
        z.append(DEEPNORM_ALPHA * _load_tile_rows(h_ref, tm, s) + y)
    out = _layer_norm(jnp.concatenate(z, axis=1), g_ref[...], b_ref[...])

    @pl.when(i < n_prompt_tiles)
    def _():
        outp_ref[...] = out

    @pl.when(i >= n_prompt_tiles)
    def _():
        outs_ref[...] = out


def _combine(ys, route_all, h_all, ln_g, ln_b, n_all, n_prompt, tm=COMBINE_TM):
    n_tiles = n_all // tm
    n_prompt_tiles = n_prompt // tm
    const = lambda a: pl.BlockSpec(a.shape, lambda i: (0,) * a.ndim)
    return pl.pallas_call(
        functools.partial(_combine_kernel, n_prompt_tiles=n_prompt_tiles),
        grid=(n_tiles,),
        in_specs=[pl.BlockSpec((TOP_K, tm * ROW_TILE, LANES), lambda i: (0, i, 0)),
                  pl.BlockSpec((tm, LANES), lambda i: (i, 0)),
                  pl.BlockSpec((tm * ROW_TILE, LANES), lambda i: (i, 0)), const(ln_g), const(ln_b)],
        out_specs=[pl.BlockSpec((tm, D_MODEL), lambda i: (jnp.minimum(i, n_prompt_tiles - 1), 0)),
                   pl.BlockSpec((tm, D_MODEL), lambda i: (jnp.maximum(i - n_prompt_tiles, 0), 0))],
        out_shape=[jax.ShapeDtypeStruct((n_prompt, D_MODEL), F32),
                   jax.ShapeDtypeStruct((n_all - n_prompt, D_MODEL), F32)],
        compiler_params=_cparams(("arbitrary",)),
        name="combine",
    )(ys, route_all, h_all, ln_g, ln_b)


def _moe(h_all, route_all, w_gate_up, b_gate_up, w_down, b_down, ln_g, ln_b, n_all, n_prompt, bm=MOE_BM):
    assert h_all.shape[0] >= (n_all + bm // TOP_K) * ROW_TILE and bm % TOP_K == 0
    n_slot = n_all * TOP_K
    flat_e = route_all[:n_all, TOP_K:2 * TOP_K].astype(I32).reshape(-1)
    order = jnp.argsort(flat_e, stable=True).astype(I32)
    counts = jnp.sum((flat_e[:, None] == jnp.arange(N_EXPERTS, dtype=I32)[None, :]).astype(I32), axis=0)
    padded = (counts + bm - 1) // bm * bm
    pad_end = jnp.cumsum(padded)
    pad_start = pad_end - padded
    start = jnp.cumsum(counts) - counts
    n_blocks = -(-(n_slot + N_EXPERTS * (bm - 1)) // bm)
    blk_row0 = jnp.arange(n_blocks, dtype=I32) * bm
    block_e = jnp.minimum(jnp.sum((pad_end[None, :] <= blk_row0[:, None]).astype(I32), axis=1), N_EXPERTS - 1)
    k = (blk_row0 - pad_start[block_e])[:, None] + jnp.arange(bm, dtype=I32)[None, :]
    valid = k < counts[block_e][:, None]
    src = jnp.clip(start[block_e][:, None] + k, 0, n_slot - 1)
    row_slot = jnp.where(valid, order[src], n_slot + jnp.arange(bm, dtype=I32)[None, :])
    ys = _experts(block_e, row_slot, h_all, w_gate_up, b_gate_up, w_down, b_down, n_all, bm)
    return _combine(ys, route_all, h_all, ln_g, ln_b, n_all, n_prompt)


def _cmp_sample_kernel(q8_ref, kvc_ref, ocmp_ref, idx_ref, *, sb, pos):
    blk = _lane((8, LANES))
    vis = blk * NSA_BLOCK + (NSA_BLOCK - 1) <= pos
    imps = []
    for s in range(sb):
        kc = kvc_ref[s, :, 0:128].astype(BF16)
        vc = kvc_ref[s, :, 128:256].astype(BF16)
        sc = jnp.where(vis, _dot_nt(q8_ref[s], kc), -jnp.inf)
        mx = jnp.max(sc, axis=1, keepdims=True)
        mx = jnp.where(mx > -jnp.inf, mx, 0.0)
        e = jnp.exp(sc - mx)
        p = e / jnp.maximum(jnp.sum(e, axis=1, keepdims=True), 1e-30)
        ocmp_ref[s] = _dot(p.astype(BF16), vc)
        for kvh in range(NSA_KV_HEADS):
            imps.append(jnp.sum(p[kvh * NSA_GROUP:(kvh + 1) * NSA_GROUP], axis=0, keepdims=True))
    imp = jnp.concatenate(imps, axis=0)
    blk2 = _lane((2 * sb, LANES))
    _, idx = _select_topk(imp, blk2 < (pos // NSA_BLOCK), blk2)
    idx_ref[0] = idx.astype(I32)


def _cmp_sample(q8c, kvc, pos, sb=SAMPLE_SB):
    n = q8c.shape[0]
    return pl.pallas_call(
        functools.partial(_cmp_sample_kernel, sb=sb, pos=pos),
        grid=(n // sb,),
        in_specs=[pl.BlockSpec((sb, 8, LANES), lambda i: (i, 0, 0)), pl.BlockSpec((sb, LANES, 256), lambda i: (i, 0, 0))],
        out_specs=[pl.BlockSpec((sb, 8, LANES), lambda i: (i, 0, 0)), pl.BlockSpec((1, 2 * sb, LANES), lambda i: (i, 0, 0))],
        out_shape=[jax.ShapeDtypeStruct((n, 8, LANES), F32), jax.ShapeDtypeStruct((n // sb, 2 * sb, LANES), I32)],
        compiler_params=_cparams(("parallel",)),
        name="cmp_sample",
    )(q8c, kvc)


def _attend_rows(s, v_t, s_self, v_self):
    mx = jnp.maximum(jnp.max(s, axis=1, keepdims=True), s_self)
    e = jnp.exp(s - mx)
    e_self = jnp.exp(s_self - mx)
    den = jnp.sum(e, axis=1, keepdims=True) + e_self
    num = _dot_nt(e.astype(BF16), v_t) + e_self.astype(BF16).astype(F32) * v_self
    return num / den


def _sample_kernel(pt_ref, sel_ref, q8r_ref, qsb_ref, qmem_ref, nsa_new_ref, win_new_ref, gate_ref, ocmp_ref,
                   cwin_ref, cmem_ref, nsa_hbm, sb_hbm, onsa_ref, osb_ref, omem_ref,
                   slc_buf, sb_buf, sem_slc, sem_sb, *, n_pages):
    i = pl.program_id(0)
    n_sel = NSA_KV_HEADS * NSA_TOPK

    def sel_block(j):
        return sel_ref[i * 2 * 16 + (j // NSA_TOPK) * 16 + j % NSA_TOPK]

    def slc_copy(j):
        page = pt_ref[i * n_pages + sel_block(j) // 2]
        return pltpu.make_async_copy(nsa_hbm.at[page, 256:512, :],
                                     slc_buf.at[j // NSA_TOPK, :, pl.ds((j % NSA_TOPK) * PAGE_SIZE, PAGE_SIZE)],
                                     sem_slc.at[0])

    def sb_copy(p):
        return pltpu.make_async_copy(sb_hbm.at[pt_ref[i * n_pages + p]], sb_buf.at[p % 2], sem_sb.at[p % 2])

    def issue(j, c):
        slc_copy(j).start()
        return c
    lax.fori_loop(0, n_sel, issue, 0)
    sb_copy(n_pages - 1).start()

    q8r = q8r_ref[0]
    row8 = _row((8, LANES))
    lane8 = _lane((8, LANES))
    head_half = (lane8 < HEAD_DIM) == (row8 < NSA_GROUP)

    kw = cwin_ref[0, 0:128, :].astype(BF16)
    vw = cwin_ref[0, 128:256, :].astype(BF16)
    k_new = win_new_ref[0, :, 0:128].astype(BF16).astype(F32)
    v_new = win_new_ref[0, :, 128:256].astype(BF16).astype(F32)
    s = jnp.where(_lane((8, NSA_WINDOW)) >= 1, _dot(q8r, kw), NEG)
    s_self = jnp.sum(q8r.astype(F32) * k_new, axis=1, keepdims=True)
    o_win = _attend_rows(s, vw, s_self, v_new)

    qm = qmem_ref[0]
    sm = _dot(qm, cmem_ref[0, 0:256, :].astype(BF16))
    em = jnp.exp(sm - jnp.max(sm, axis=1, keepdims=True))
    pm = em / jnp.sum(em, axis=1, keepdims=True)
    omem_ref[0] = _dot_nt(pm.astype(BF16), cmem_ref[0, 256:512, :].astype(BF16))

    def wait_slc(j, c):
        slc_copy(j).wait()
        return c
    lax.fori_loop(0, n_sel, wait_slc, 0)
    k_new = nsa_new_ref[0, :, 256:384].astype(BF16).astype(F32)
    v_new = nsa_new_ref[0, :, 384:512].astype(BF16).astype(F32)
    s_self = jnp.sum(q8r.astype(F32) * k_new, axis=1, keepdims=True)
    o_k = []
    for kvh in range(NSA_KV_HEADS):
        in_block = jnp.concatenate(
            [(lane8 // NSA_BLOCK) == (sel_block(kvh * NSA_TOPK + j) % 2) for j in range(NSA_TOPK)], axis=1)
        sc = jnp.where(in_block, _dot(q8r, slc_buf[kvh, 0:128, :].astype(BF16)), NEG)
        o_k.append(_attend_rows(sc, slc_buf[kvh, 128:256, :].astype(BF16), s_self, v_new))
    o_slc = jnp.where(row8 < NSA_GROUP, o_k[0], o_k[1])
    gate = gate_ref[0]
    merged = gate[:, 0:1] * ocmp_ref[0] + gate[:, 1:2] * o_slc + gate[:, 2:3] * o_win
    onsa_ref[0] = jnp.where(head_half, merged, 0.0)

    qs = qsb_ref[0]
    u2 = _suffix_matrix(PAGE_SIZE)
    all_keys = jnp.full((8, PAGE_SIZE), True)

    def cond(state):
        p, top, _, _ = state
        return (p >= 0) & (top > SB_EXIT)

    def body(state):
        p, _, carry, acc = state
        sb_copy(p).wait()

        @pl.when(p > 0)
        def _():
            sb_copy(p - 1).start()
        a, carry = _sb_tile(_dot(qs, sb_buf[p % 2, 0:256, :].astype(BF16)), all_keys, carry, u2)
        contrib = _dot_nt(a, sb_buf[p % 2, 256:512, :].astype(BF16))
        return p - 1, jnp.max(carry[0:SB_HEADS]), carry, acc + contrib

    p_end, _, _, acc = lax.while_loop(
        cond, body, (n_pages - 1, jnp.zeros((), F32), jnp.zeros((8, LANES), F32), jnp.zeros((8, 256), F32)))

    @pl.when(p_end >= 0)
    def _():
        sb_copy(p_end).wait()
    osb_ref[0] = acc


def _sample_attend(page_table, sel, q8r, qsb8, qmem8, nsa_new, win_new, gate8, ocmp8, cache_win, cache_mem,
                   cache_nsa, cache_sb):
    n = q8r.shape[0]
    n_pages = page_table.shape[0] // n
    per = lambda a: pl.BlockSpec((1,) + a.shape[1:], lambda i, pt, sl: (i,) + (0,) * (a.ndim - 1))
    vm = [q8r, qsb8, qmem8, nsa_new, win_new, gate8, ocmp8, cache_win, cache_mem]
    return pl.pallas_call(
        functools.partial(_sample_kernel, n_pages=n_pages),
        grid_spec=pltpu.PrefetchScalarGridSpec(
            num_scalar_prefetch=2,
            grid=(n,),
            in_specs=[per(a) for a in vm] + [pl.BlockSpec(memory_space=pl.ANY)] * 2,
            out_specs=[pl.BlockSpec((1, 8, LANES), lambda i, pt, sl: (i, 0, 0)),
                       pl.BlockSpec((1, 8, 256), lambda i, pt, sl: (i, 0, 0)),
                       pl.BlockSpec((1, 8, 256), lambda i, pt, sl: (i, 0, 0))],
            scratch_shapes=[pltpu.VMEM((NSA_KV_HEADS, 256, NSA_TOPK * PAGE_SIZE), F32),
                            pltpu.VMEM((2, 512, PAGE_SIZE), F32),
                            pltpu.SemaphoreType.DMA((1,)), pltpu.SemaphoreType.DMA((2,))],
        ),
        out_shape=[jax.ShapeDtypeStruct((n, 8, LANES), F32), jax.ShapeDtypeStruct((n, 8, 256), F32),
                   jax.ShapeDtypeStruct((n, 8, 256), F32)],
        compiler_params=_cparams(("arbitrary",)),
        name="sample_attend",
    )(page_table, sel, *vm, cache_nsa, cache_sb)


def _prep_mixer_weights(w_in, pe_cmp, w_cmp1, b_cmp1, w_cmp2, b_cmp2, w_out):
    head_perm = np.array([kvh * NSA_GROUP + g for g in range(NSA_GROUP) for kvh in range(NSA_KV_HEADS)])
    qcols = (head_perm[:, None] * HEAD_DIM + np.arange(HEAD_DIM)[None, :]).reshape(-1)
    w_q = w_in[:, 0:512][:, qcols]
    w_kv = w_in[:, 512:1280]
    w_g = jnp.pad(w_in[:, 1280:1304], ((0, 0), (0, LANES - 24)))
    w_all = jnp.concatenate([w_q, w_kv, w_in[:, 1304:2328], w_g], axis=1).astype(BF16)
    w_out_p = jnp.concatenate([w_out[0:512][qcols], w_out[512:]], axis=0).astype(BF16)
    w1 = jnp.transpose(w_cmp1.reshape(2, NSA_BLOCK, HEAD_DIM, CMP_HIDDEN), (0, 2, 1, 3))
    w1bd = jnp.zeros((2, HEAD_DIM, 2, NSA_BLOCK, 2, CMP_HIDDEN), F32)
    w2bd = jnp.zeros((2, 2, CMP_HIDDEN, 2, HEAD_DIM), F32)
    for blk in range(2):
        w1bd = w1bd.at[:, :, blk, :, blk, :].set(w1)
        w2bd = w2bd.at[:, blk, :, blk, :].set(w_cmp2)
    n_groups = HEAD_DIM // COMPRESS_DG
    w1bd = w1bd.reshape(2, n_groups, COMPRESS_DG * LANES, 2 * CMP_HIDDEN).astype(BF16)
    w2bd = w2bd.reshape(2, 2 * CMP_HIDDEN, LANES).astype(BF16)
    pe_t = jnp.tile(jnp.transpose(pe_cmp, (0, 2, 1)), (1, 1, 2)).reshape(2, n_groups, 1, COMPRESS_DG * LANES)
    b1t = jnp.tile(b_cmp1, (1, 2)).reshape(2, 1, 2 * CMP_HIDDEN)
    b2t = jnp.tile(b_cmp2, (1, 2)).reshape(2, 1, LANES)
    return w_all, w_out_p, (pe_t, w1bd, b1t, w2bd, b2t)


def _rope_tables(pos):
    half = HEAD_DIM // 2
    inv = ROPE_THETA ** (-jnp.arange(half, dtype=F32) / half)
    ang = pos.astype(F32)[:, None] * inv[None, :]
    cos, sin = jnp.cos(ang), jnp.sin(ang)
    cos_t = jnp.concatenate([cos, cos, cos, cos], axis=1)
    sin_t = jnp.concatenate([-sin, sin, -sin, sin], axis=1)
    return cos_t, sin_t


def _head_rows(q, n_heads, width):
    n = q.shape[0]
    lane_head = (np.arange(width) // HEAD_DIM)[None, None, :]
    rows = jnp.where(lane_head == np.arange(n_heads)[None, :, None], q[:, None, :], jnp.zeros((), q.dtype))
    return jnp.pad(rows, ((0, 0), (0, 8 - n_heads), (0, 0)))


def _nsa_head_rows(q):
    n = q.shape[0]
    q4 = q.reshape(n, NSA_GROUP, LANES)
    low = (np.arange(LANES) < HEAD_DIM)[None, None, :]
    zero = jnp.zeros((), q.dtype)
    return jnp.concatenate([jnp.where(low, q4, zero), jnp.where(low, zero, q4)], axis=1)


def _diag_heads(o, n_heads):
    lane_head = (np.arange(o.shape[2]) // HEAD_DIM)[None, None, :]
    keep = lane_head == np.arange(8)[None, :, None]
    return jnp.sum(jnp.where(keep, o, 0.0), axis=1)


def kernel(x_prompt, x_sample, mem_prompt, cache_nsa, cache_sb, cache_win, cache_mem, page_table, w_in, pe_cmp, w_cmp1,
           b_cmp1, w_cmp2, b_cmp2, w_mem_kv, w_out, ln1_g, ln1_b, w_router, b_router, w_gate_up, b_gate_up, w_down,
           b_down, ln2_g, ln2_b):
    assert w_in.shape[0] == DEPTH
    b, t, d = x_prompt.shape
    db = x_sample.shape[0]
    n = b * t
    n_pages = page_table.shape[1]
    n_phys = cache_nsa.shape[1]
    past = n_pages * PAGE_SIZE
    w_all, w_out_p, cw = _prep_mixer_weights(w_in[0], pe_cmp[0], w_cmp1[0], b_cmp1[0], w_cmp2[0], b_cmp2[0], w_out[0])
    ln1 = (ln1_g[0].reshape(1, d), ln1_b[0].reshape(1, d))
    ln2 = (ln2_g[0].reshape(1, d), ln2_b[0].reshape(1, d))
    wr = jnp.pad(w_router[0], ((0, 0), (0, LANES - N_EXPERTS)))
    br = jnp.concatenate([b_router[0], jnp.full((LANES - N_EXPERTS,), NEG, F32)]).reshape(1, LANES)

    cos_t, sin_t = _rope_tables(jnp.arange(t, dtype=I32))
    qc, qr, nsa_t, win_t, sb_t, slc_b, win_b, sb_b, qsb, qmem, gate = _project(x_prompt.reshape(n, d), w_all, cos_t,
                                                                               sin_t, b, t)
    kvc = _blocks_from_compress(_compress(nsa_t, jnp.arange(n // PAGE_SIZE, dtype=I32), *cw), b)
    ocmp, bias = _cmp_attend(qc.reshape(b, t, 512), kvc)
    onsa = _nsa_attend(qr.reshape(b, t, 512), bias, ocmp, gate, slc_b.reshape(b, t, 256), win_b.reshape(b, t, 256))
    osb = _sb_attend(qsb.reshape(b, t, 256), sb_b.reshape(b, t, 512))
    mkv, mkv_t = _memkv(mem_prompt, w_mem_kv[0].astype(BF16))
    omem = _mem_attend(qmem.reshape(b, t, 256), mkv)

    feature_major = lambda c: jnp.transpose(c[0], (0, 2, 3, 4, 1)).reshape(c.shape[1], -1, c.shape[2])
    cos_s, sin_s = _rope_tables(jnp.full((db,), past, I32))
    qc_s, qr_s, nsa_ts, win_ts, sb_ts, _, _, _, qsb_s, qmem_s, gate_s = _project(x_sample.reshape(db, d), w_all, cos_s,
                                                                               sin_s, 1, db)
    nsa_s, win_s = nsa_ts[0].T, win_ts[0].T
    pages_nsa = feature_major(cache_nsa)
    pages_sb = feature_major(cache_sb)
    pt_flat = page_table.reshape(-1)
    kvc_s = _blocks_from_compress(_compress(pages_nsa, pt_flat, *cw), db)
    ocmp8, idx = _cmp_sample(_nsa_head_rows(qc_s), kvc_s, past)
    sel = idx[:, :, :16].reshape(-1)
    gate8 = jnp.pad(gate_s[:, :24].reshape(db, 8, 3), ((0, 0), (0, 0), (0, LANES - 3)))
    onsa8, osb8, omem8 = _sample_attend(
        pt_flat, sel, _nsa_head_rows(qr_s), _head_rows(qsb_s, SB_HEADS, 256), _head_rows(qmem_s, MEM_HEADS, 256),
        nsa_s.reshape(db, 1, 512), win_s.reshape(db, 1, 256), gate8, ocmp8,
        feature_major(cache_win), feature_major(cache_mem), pages_nsa, pages_sb)
    onsa_s = (onsa8[:, :NSA_GROUP] + onsa8[:, NSA_GROUP:]).reshape(db, 512)
    tail = _outproj(onsa_s, _diag_heads(osb8, SB_HEADS), _diag_heads(omem8, MEM_HEADS), x_sample.reshape(db, d),
                    w_out_p, *ln1, wr, br)

    h_all, route_all = _outproj(onsa, osb, omem, x_prompt.reshape(n, d), w_out_p, *ln1, wr, br, tail=tail)
    yp, ys = _moe(h_all, route_all, w_gate_up[0], b_gate_up[0], w_down[0], b_down[0], *ln2, n + db, n)

    def rows(x_t, kinds, heads):
        s, _, p = x_t.shape
        return jnp.transpose(x_t.reshape(s, kinds, heads, HEAD_DIM, p), (0, 4, 1, 2, 3))[None]

    win_len = min(NSA_WINDOW, t)
    win_all_t = jnp.concatenate([feature_major(cache_win), win_s[:, :, None]], axis=2)[:, :, 1:]
    return (yp.reshape(b, t, d), ys.reshape(db, 1, d),
            rows(nsa_t, 4, NSA_KV_HEADS), rows(win_t[:, :, t - win_len:], 2, NSA_KV_HEADS),
            rows(sb_t, 2, SB_HEADS), rows(mkv_t, 2, MEM_HEADS),
            jnp.transpose(rows(nsa_ts, 4, NSA_KV_HEADS), (0, 2, 1, 3, 4, 5)), rows(win_all_t, 2, NSA_KV_HEADS),
            jnp.transpose(rows(sb_ts, 2, SB_HEADS), (0, 2, 1, 3, 4, 5)))
```
